```python
import jax, jax.numpy as jnp
from jax import lax
import numpy as np

D_MODEL = 1024
BATCH = 2
SEQ = 16384
DEPTH = 2

N_META = 16
GRID_W = 64
BRANCH_W = D_MODEL // 2
D_MIX = 3 * BRANCH_W
NORM_EPS = 1e-6
NA_HD = 64
NA_HEADS = BRANCH_W // NA_HD
WIN_R = 8
WIN_C = 16
GLA_HEADS = 4
GLA_DV = BRANCH_W // GLA_HEADS
GLA_DK = GLA_DV // 2
GLA_KW = GLA_HEADS * GLA_DK
GLA_LR = 16
GLA_NORMALIZER = 16.0
GLA_CHUNK = 64
GLA_NORM_EPS = 1e-5
RW_N = 64
RW_HEADS = BRANCH_W // RW_N
RW_W_LORA = 64
RW_A_LORA = 64
RW_V_LORA = 32
RW_GN_EPS = 64e-5
CONV_W = 3
NA_COLS = 4 * BRANCH_W
GLA_COLS = 2 * GLA_KW + 2 * BRANCH_W + 2 * GLA_LR
RW_SHIFT_COLS = 3 * BRANCH_W + 2 * RW_W_LORA + 2 * RW_A_LORA
RW_COLS = RW_SHIFT_COLS + BRANCH_W
P_IN = NA_COLS + GLA_COLS + RW_COLS

kernel_name = 'hymba_na_gla_rwkv7_bidir_encoder'


def rms_norm(x, g, eps=NORM_EPS):
    x32 = x.astype(jnp.float32)
    y = x32 * lax.rsqrt(jnp.mean(x32 * x32, axis=-1, keepdims=True) + eps)
    return (y * g.astype(jnp.float32)).astype(x.dtype)


def _split(z, widths):
    bounds = [int(b) for b in np.cumsum(widths)[:-1]]
    return jnp.split(z, bounds, axis=-1)


def _reorder_bwd(z):
    return jnp.concatenate([z[:, :N_META], jnp.flip(z[:, N_META:], axis=1)], axis=1)


def _to_dirs(z_fwd, z_bwd):
    return jnp.stack([z_fwd, _reorder_bwd(z_bwd)], axis=0)


def _sum_dirs(y):
    return y[0] + _reorder_bwd(y[1])


def centred_dwconv(z, w):
    c = z.shape[-1]
    return lax.conv_general_dilated(
        z, w[:, None, :].astype(z.dtype), window_strides=(1,),
        padding=[(CONV_W // 2, CONV_W // 2)],
        dimension_numbers=('NWC', 'WIO', 'NWC'), feature_group_count=c)


def neighbourhood_attention(q, k, v, rpb):
    B, L, H, hd = q.shape
    T = L - N_META
    rows = T // GRID_W
    wr = min(WIN_R, rows)
    scale = hd ** -0.5
    qm, km, vm = q[:, :N_META], k[:, :N_META], v[:, :N_META]
    to_grid = lambda a: a[:, N_META:].reshape(B, rows, GRID_W, H, hd).transpose(0, 3, 1, 2, 4)
    qg, kg, vg = to_grid(q), to_grid(k), to_grid(v)
    km_h, vm_h = km.transpose(0, 2, 1, 3), vm.transpose(0, 2, 1, 3)
    cols = np.arange(GRID_W)
    cstart = np.clip(cols - WIN_C // 2, 0, GRID_W - WIN_C)
    col_idx = cstart[:, None] + np.arange(WIN_C)[None, :]
    dc = col_idx - cols[:, None] + (WIN_C - 1)

    def row_block(r):
        rs = jnp.clip(r - wr // 2, 0, rows - wr)
        qr = lax.dynamic_index_in_dim(qg, r, axis=2, keepdims=False)
        kb = lax.dynamic_slice_in_dim(kg, rs, wr, axis=2)
        vb = lax.dynamic_slice_in_dim(vg, rs, wr, axis=2)
        kw = kb[:, :, :, col_idx]
        vw = vb[:, :, :, col_idx]
        s_loc = jnp.einsum('bhcd,bhrckd->bhcrk', qr, kw).astype(jnp.float32) * scale
        dr = rs + jnp.arange(wr) - r + (WIN_R - 1)
        bias = rpb[:, dr][:, :, dc]
        s_loc = s_loc + bias.transpose(0, 2, 1, 3)[None].astype(jnp.float32)
        s_meta = jnp.einsum('bhcd,bhmd->bhcm', qr, km_h).astype(jnp.float32) * scale
        s = jnp.concatenate([s_loc.reshape(B, H, GRID_W, wr * WIN_C), s_meta], axis=-1)
        p = jax.nn.softmax(s, axis=-1).astype(v.dtype)
        p_loc = p[..., :wr * WIN_C].reshape(B, H, GRID_W, wr, WIN_C)
        p_meta = p[..., wr * WIN_C:]
        return (jnp.einsum('bhcrk,bhrckd->bhcd', p_loc, vw)
                + jnp.einsum('bhcm,bhmd->bhcd', p_meta, vm_h))

    og = lax.map(row_block, jnp.arange(rows))
    og = og.transpose(1, 0, 3, 2, 4).reshape(B, T, H, hd)
    s_mm = jnp.einsum('bmhd,bnhd->bhmn', qm, km).astype(jnp.float32) * scale
    p_mm = jax.nn.softmax(s_mm, axis=-1).astype(v.dtype)
    om = jnp.einsum('bhmn,bnhd->bmhd', p_mm, vm)
    return jnp.concatenate([om, og], axis=1)


def na_branch(z, rpb):
    B, L, _ = z.shape
    q, k, v, g = _split(z, (BRANCH_W,) * 4)
    heads = lambda a: a.reshape(B, L, NA_HEADS, NA_HD)
    o = neighbourhood_attention(heads(q), heads(k), heads(v), rpb)
    return o.reshape(B, L, BRANCH_W) * jax.nn.silu(g)


def gla_chunked(q, k, v, g):
    Dn, B, L, H, dk = q.shape
    dv = v.shape[-1]
    C = GLA_CHUNK
    pad = (-N_META) % C
    padf = lambda a: jnp.pad(a, ((0, 0), (0, 0), (pad, 0), (0, 0), (0, 0)))
    nc = (L + pad) // C
    ch = lambda a: padf(a).reshape(Dn, B, nc, C, H, a.shape[-1]).transpose(0, 1, 4, 2, 3, 5)
    q, k, v, g = ch(q), ch(k), ch(v), ch(g)
    b = jnp.cumsum(g, axis=-2)
    qe = q * jnp.exp(b)
    ke = k * jnp.exp(-b)
    kd = k * jnp.exp(b[..., -1:, :] - b)
    mask = np.tril(np.ones((C, C), dtype=bool))
    A = jnp.where(mask, jnp.einsum('zbhnid,zbhnjd->zbhnij', qe, ke), 0.0)
    o = jnp.einsum('zbhnij,zbhnjv->zbhniv', A, v)
    dS = jnp.einsum('zbhnjd,zbhnjv->zbhndv', kd, v)
    dec = jnp.exp(b[..., -1, :])

    def step(S, inp):
        dS_n, dec_n = inp
        return dec_n[..., None] * S + dS_n, S

    S0 = jnp.zeros((Dn, B, H, dk, dv), jnp.float32)
    _, S_prev = lax.scan(step, S0, (jnp.moveaxis(dS, 3, 0), jnp.moveaxis(dec, 3, 0)))
    S_prev = jnp.moveaxis(S_prev, 0, 3)
    o = o + jnp.einsum('zbhnid,zbhndv->zbhniv', qe, S_prev)
    o = o.transpose(0, 1, 3, 4, 2, 5).reshape(Dn, B, nc * C, H, dv)
    return o[:, :, pad:]


def gla_branch(z, g_up, g_b, norm_g):
    B, L, _ = z.shape
    f32 = jnp.float32
    q, k, v, g, gd = _split(z.astype(f32), (GLA_KW, GLA_KW, BRANCH_W, BRANCH_W, 2 * GLA_LR))
    logits = jnp.einsum('bldr,drc->dblc', gd.reshape(B, L, 2, GLA_LR), g_up.astype(f32)) + g_b[:, None, None, :]
    gk = jax.nn.log_sigmoid(logits) / GLA_NORMALIZER
    hk = lambda a: a.reshape(a.shape[:-1] + (GLA_HEADS, -1))
    q = q * GLA_DK ** -0.5
    o = gla_chunked(hk(_to_dirs(q, q)), hk(_to_dirs(k, k)), hk(_to_dirs(v, v)),
                    hk(_to_dirs(gk[0], gk[1])))
    o = _sum_dirs(o)
    o = o * lax.rsqrt(jnp.mean(o * o, axis=-1, keepdims=True) + GLA_NORM_EPS) * norm_g
    return (o.reshape(B, L, BRANCH_W) * jax.nn.silu(g)).astype(z.dtype)


def rwkv7_scan(r, w, k, v, a, b):
    def step(S, inp):
        r_t, w_t, k_t, v_t, a_t, b_t = inp
        sa = jnp.einsum('zbhvk,zbhk->zbhv', S, a_t)
        S = (S * w_t[..., None, :] + sa[..., :, None] * b_t[..., None, :]
             + v_t[..., :, None] * k_t[..., None, :])
        return S, jnp.einsum('zbhvk,zbhk->zbhv', S, r_t)

    Dn, B, L, H, N = r.shape
    S0 = jnp.zeros((Dn, B, H, N, N), jnp.float32)
    xs = tuple(jnp.moveaxis(t, 2, 0) for t in (r, w, k, v, a, b))
    _, y = lax.scan(step, S0, xs)
    return jnp.moveaxis(y, 0, 2)


def rwkv7_branch(z, v_first, conv_w, w0, w_up, a0, a_up, k_k, k_a, r_k, ln_g, ln_b, v_mix):
    B, L, _ = z.shape
    f32 = jnp.float32
    zs = centred_dwconv(z[..., :RW_SHIFT_COLS], conv_w).astype(f32)
    gate = z[..., RW_SHIFT_COLS:].astype(f32)
    r, k, v, wd, ad = _split(zs, (BRANCH_W,) * 3 + (2 * RW_W_LORA, 2 * RW_A_LORA))
    w_lora = jnp.einsum('bldr,drc->dblc', jnp.tanh(wd.reshape(B, L, 2, RW_W_LORA)), w_up) + w0[:, None, None, :]
    decay = jnp.exp(-jnp.exp(-jax.nn.softplus(-w_lora) - 0.5))
    alpha = jax.nn.sigmoid(jnp.einsum('bldr,drc->dblc', ad.reshape(B, L, 2, RW_A_LORA), a_up) + a0[:, None, None, :])
    hd = lambda a: a.reshape(a.shape[:-1] + (RW_HEADS, RW_N))
    kk = hd(k * k_k)
    kk = kk / jnp.maximum(jnp.sqrt(jnp.sum(kk * kk, axis=-1, keepdims=True)), 1e-12)
    kk = kk.reshape(B, L, BRANCH_W)
    k_mod = k * (1.0 + (alpha - 1.0) * k_a)
    if v_mix is None:
        v_first = v
    else:
        v0, v_down, v_up = v_mix
        v = v + (v_first - v) * jax.nn.sigmoid(v0 + (v @ v_down) @ v_up)
    y = rwkv7_scan(hd(_to_dirs(r, r)), hd(_to_dirs(decay[0], decay[1])),
                   hd(_to_dirs(k_mod[0], k_mod[1])), hd(_to_dirs(v, v)),
                   hd(_to_dirs(-kk, -kk)), hd(_to_dirs(kk * alpha[0], kk * alpha[1])))
    y = _sum_dirs(y)
    mu = jnp.mean(y, axis=-1, keepdims=True)
    var = jnp.mean(jnp.square(y - mu), axis=-1, keepdims=True)
    y = ((y - mu) * lax.rsqrt(var + RW_GN_EPS)).reshape(B, L, BRANCH_W) * ln_g + ln_b
    k_bonus = 0.5 * (k_mod[0] + k_mod[1])
    bonus = jnp.sum(hd(r * k_bonus * r_k), axis=-1, keepdims=True) * hd(v)
    y = y + bonus.reshape(B, L, BRANCH_W)
    return (y * jax.nn.silu(gate)).astype(z.dtype), v_first


def setup_inputs(seed: int = 0) -> dict:
    key = jax.random.key(seed)
    ks = jax.random.split(key, 24)
    nrm = lambda k, shape, s: jax.random.normal(k, shape, jnp.float32) * s
    dm1 = DEPTH - 1
    conv_base = jnp.array([0.25, 0.5, 0.25], jnp.float32)[None, :, None]
    return {
        'x': nrm(ks[0], (BATCH, SEQ, D_MODEL), 1.0),
        'meta': nrm(ks[1], (N_META, D_MODEL), 1.0),
        'norm_g': 1.0 + nrm(ks[2], (DEPTH, D_MODEL), 0.02),
        'w_in': nrm(ks[3], (DEPTH, D_MODEL, P_IN), D_MODEL ** -0.5),
        'w_out': nrm(ks[4], (DEPTH, D_MIX, D_MODEL), D_MIX ** -0.5),
        'na_rpb': nrm(ks[5], (DEPTH, NA_HEADS, 2 * WIN_R - 1, 2 * WIN_C - 1), 0.1),
        'gla_g_up': nrm(ks[6], (DEPTH, 2, GLA_LR, GLA_KW), GLA_LR ** -0.5),
        'gla_g_b': nrm(ks[7], (DEPTH, 2, GLA_KW), 0.1),
        'gla_norm_g': 1.0 + nrm(ks[8], (DEPTH, GLA_DV), 0.02),
        'rw_conv': conv_base + nrm(ks[9], (DEPTH, CONV_W, RW_SHIFT_COLS), 0.05),
        'rw_w0': jax.random.uniform(ks[10], (DEPTH, 2, BRANCH_W), jnp.float32, -4.0, 1.0),
        'rw_w_up': nrm(ks[11], (DEPTH, 2, RW_W_LORA, BRANCH_W), 0.5 * RW_W_LORA ** -0.5),
        'rw_a0': nrm(ks[12], (DEPTH, 2, BRANCH_W), 0.1),
        'rw_a_up': nrm(ks[13], (DEPTH, 2, RW_A_LORA, BRANCH_W), 0.5 * RW_A_LORA ** -0.5),
        'rw_k_k': 0.85 + nrm(ks[14], (DEPTH, BRANCH_W), 0.05),
        'rw_k_a': 1.0 + nrm(ks[15], (DEPTH, BRANCH_W), 0.05),
        'rw_r_k': nrm(ks[16], (DEPTH, BRANCH_W), 0.1),
        'rw_ln_g': 1.0 + nrm(ks[17], (DEPTH, BRANCH_W), 0.02),
        'rw_ln_b': nrm(ks[18], (DEPTH, BRANCH_W), 0.02),
        'rw_v0': 1.0 + nrm(ks[19], (dm1, BRANCH_W), 0.1),
        'rw_v_down': nrm(ks[20], (dm1, BRANCH_W, RW_V_LORA), BRANCH_W ** -0.5),
        'rw_v_up': nrm(ks[21], (dm1, RW_V_LORA, BRANCH_W), 0.5 * RW_V_LORA ** -0.5),
        'final_norm_g': 1.0 + nrm(ks[22], (D_MODEL,), 0.02),
    }


def reference(x, meta, norm_g, w_in, w_out, na_rpb, gla_g_up, gla_g_b, gla_norm_g,
              rw_conv, rw_w0, rw_w_up, rw_a0, rw_a_up, rw_k_k, rw_k_a, rw_r_k,
              rw_ln_g, rw_ln_b, rw_v0, rw_v_down, rw_v_up, final_norm_g):
    B, _, D = x.shape
    h = jnp.concatenate([jnp.broadcast_to(meta[None].astype(x.dtype), (B, N_META, D)), x], axis=1)
    v_first = None
    for l in range(DEPTH):
        hn = rms_norm(h, norm_g[l])
        w = w_in[l]
        z_na = hn @ w[:, :NA_COLS]
        z_gla = hn @ w[:, NA_COLS:NA_COLS + GLA_COLS]
        z_rw = hn @ w[:, NA_COLS + GLA_COLS:]
        o_na = na_branch(z_na, na_rpb[l])
        o_gla = gla_branch(z_gla, gla_g_up[l], gla_g_b[l], gla_norm_g[l])
        v_mix = None if l == 0 else (rw_v0[l - 1], rw_v_down[l - 1], rw_v_up[l - 1])
        o_rw, v_first = rwkv7_branch(z_rw, v_first, rw_conv[l], rw_w0[l], rw_w_up[l], rw_a0[l],
                                     rw_a_up[l], rw_k_k[l], rw_k_a[l], rw_r_k[l],
                                     rw_ln_g[l], rw_ln_b[l], v_mix)
        h = h + jnp.concatenate([o_na, o_gla, o_rw], axis=-1) @ w_out[l]
    return rms_norm(h, final_norm_g)[:, N_META:]
```

```python
import functools
import math

import numpy as np
import jax
import jax.numpy as jnp
from jax import lax
from jax.experimental import pallas as pl
from jax.experimental.pallas import tpu as pltpu

F32 = jnp.float32
BF16 = jnp.bfloat16

D_MODEL = 1024
N_META = 16
GRID_W = 64
BRANCH_W = D_MODEL // 2
D_MIX = 3 * BRANCH_W
NORM_EPS = 1e-6
NA_HD = 64
NA_HEADS = BRANCH_W // NA_HD
WIN_R = 8
WIN_C = 16
GLA_HEADS = 4
GLA_DV = BRANCH_W // GLA_HEADS
GLA_DK = GLA_DV // 2
GLA_KW = GLA_HEADS * GLA_DK
GLA_LR = 16
GLA_NORMALIZER = 16.0
GLA_NORM_EPS = 1e-5
RW_N = 64
RW_HEADS = BRANCH_W // RW_N
RW_W_LORA = 64
RW_A_LORA = 64
RW_V_LORA = 32
RW_GN_EPS = 64e-5
NA_COLS = 4 * BRANCH_W
GLA_COLS = 2 * GLA_KW + 2 * BRANCH_W + 2 * GLA_LR
RW_SHIFT_COLS = 3 * BRANCH_W + 2 * RW_W_LORA + 2 * RW_A_LORA
RW_COLS = RW_SHIFT_COLS + BRANCH_W

LANES = 128
ROW_TILE = 512
CHUNK = 64
PAD = ROW_TILE
META_ROW0 = PAD - N_META
META_CHUNK = META_ROW0 // CHUNK
META_LOCAL0 = META_ROW0 - META_CHUNK * CHUNK
GLA_ZCOLS = 2 * GLA_KW + 2 * BRANCH_W + LANES
RW_LORA_COLS = 2 * RW_W_LORA + 2 * RW_A_LORA
NEG_BIG = -1e30
VMEM_LIMIT = 56 * 1024 * 1024

_NT = (((1,), (1,)), ((), ()))
_TN = (((0,), (0,)), ((), ()))


def _mm(a, b):
    return jnp.dot(a.astype(BF16), b.astype(BF16), preferred_element_type=F32)


def _mm_nt(a, b):
    return lax.dot_general(a.astype(BF16), b.astype(BF16), _NT, preferred_element_type=F32)


def _mm_tn(a, b):
    return lax.dot_general(a.astype(BF16), b.astype(BF16), _TN, preferred_element_type=F32)


def _split2(x):
    hi = x.astype(BF16)
    lo = (x - hi.astype(F32)).astype(BF16)
    return hi, lo


def _split3(x):
    hi = x.astype(BF16)
    r = x - hi.astype(F32)
    mid = r.astype(BF16)
    lo = (r - mid.astype(F32)).astype(BF16)
    return hi, mid, lo


def _mm3(a, b):
    ah, al = _split2(a)
    bh, bl = _split2(b)
    d = functools.partial(jnp.dot, preferred_element_type=F32)
    return d(ah, bh) + (d(ah, bl) + d(al, bh))


def _mm_exact_lhs(a_bf16, x):
    xh, xm, xl = _split3(x)
    d = functools.partial(jnp.dot, preferred_element_type=F32)
    return d(a_bf16, xh) + (d(a_bf16, xm) + d(a_bf16, xl))


def _mm_exact_rhs(x, b_bf16):
    xh, xm, xl = _split3(x)
    d = functools.partial(jnp.dot, preferred_element_type=F32)
    return d(xh, b_bf16) + (d(xm, b_bf16) + d(xl, b_bf16))


def _sigmoid(x):
    return 1.0 / (1.0 + jnp.exp(-x))


def _silu(x):
    return x * _sigmoid(x)


def _softplus(x):
    return jnp.maximum(x, 0.0) + jnp.log(1.0 + jnp.exp(-jnp.abs(x)))


def _cparams(sem):
    return pltpu.CompilerParams(dimension_semantics=sem, vmem_limit_bytes=VMEM_LIMIT)


def _inproj_kernel(h_ref, g_ref, wna_ref, wgla_ref, wrw_ref, zna_ref, zgla_ref, zrw_ref):
    h = h_ref[...]
    ms = jnp.mean(h * h, axis=-1, keepdims=True)
    hn = (h * lax.rsqrt(ms + NORM_EPS) * g_ref[...]).astype(BF16)
    zna_ref[...] = jnp.dot(hn, wna_ref[...], preferred_element_type=F32)
    zgla_ref[...] = jnp.dot(hn, wgla_ref[...], preferred_element_type=F32)
    zrw_ref[...] = jnp.dot(hn, wrw_ref[...], preferred_element_type=F32)


def _in_proj(h2d, norm_g, wna, wgla, wrw):
    n = h2d.shape[0]
    tm = ROW_TILE // 2
    const = lambda i: (0, 0)
    row = lambda i: (i, 0)
    return pl.pallas_call(
        _inproj_kernel,
        grid=(n // tm,),
        in_specs=[
            pl.BlockSpec((tm, D_MODEL), row),
            pl.BlockSpec((1, D_MODEL), const),
            pl.BlockSpec(wna.shape, const),
            pl.BlockSpec(wgla.shape, const),
            pl.BlockSpec(wrw.shape, const),
        ],
        out_specs=[
            pl.BlockSpec((tm, wna.shape[1]), row),
            pl.BlockSpec((tm, wgla.shape[1]), row),
            pl.BlockSpec((tm, wrw.shape[1]), row),
        ],
        out_shape=[
            jax.ShapeDtypeStruct((n, wna.shape[1]), F32),
            jax.ShapeDtypeStruct((n, wgla.shape[1]), F32),
            jax.ShapeDtypeStruct((n, wrw.shape[1]), F32),
        ],
        compiler_params=_cparams(("parallel",)),
        name="in_proj",
    )(h2d, norm_g, wna, wgla, wrw)


NA_QROWS = ROW_TILE // GRID_W


def _na_kernel(q_ref, g_ref, kp_ref, kc_ref, kn_ref, vp_ref, vc_ref, vn_ref, km_ref, vm_ref,
               bias_ref, o_ref, ks_ref, vs_ref, *, rows):
    i = pl.program_id(1)
    scale = NA_HD ** -0.5
    km = km_ref[0].astype(BF16)
    vm = vm_ref[0].astype(BF16)

    @pl.when(i == 0)
    def _meta_tile():
        o_ref[0, 0:META_ROW0, :] = jnp.zeros((META_ROW0, BRANCH_W), F32)
        qm = (q_ref[0, META_ROW0:PAD, :] * scale).astype(BF16)
        outs = []
        for h in range(NA_HEADS):
            sl = slice(h * NA_HD, (h + 1) * NA_HD)
            s = lax.dot_general(qm[:, sl], km[:, sl], _NT, preferred_element_type=F32)
            m = jnp.max(s, axis=-1, keepdims=True)
            p = jnp.exp(s - m)
            den = jnp.sum(p, axis=-1, keepdims=True)
            outs.append(jnp.dot(p.astype(BF16), vm[:, sl], preferred_element_type=F32) / den)
        om = jnp.concatenate(outs, axis=-1)
        o_ref[0, META_ROW0:PAD, :] = om * _silu(g_ref[0, META_ROW0:PAD, :])

    @pl.when(i > 0)
    def _real_tile():
        ks_ref[0:ROW_TILE, :] = kp_ref[0].astype(BF16)
        ks_ref[ROW_TILE:2 * ROW_TILE, :] = kc_ref[0].astype(BF16)
        ks_ref[2 * ROW_TILE:3 * ROW_TILE, :] = kn_ref[0].astype(BF16)
        vs_ref[0:ROW_TILE, :] = vp_ref[0].astype(BF16)
        vs_ref[ROW_TILE:2 * ROW_TILE, :] = vc_ref[0].astype(BF16)
        vs_ref[2 * ROW_TILE:3 * ROW_TILE, :] = vn_ref[0].astype(BF16)
        r0 = (i - 1) * NA_QROWS

        def row_body(j, carry):
            r = r0 + j
            rs = jnp.clip(r - WIN_R // 2, 0, rows - WIN_R)
            start = pl.multiple_of((rs - r0 + NA_QROWS) * GRID_W, GRID_W)
            pat = r - rs
            qoff = pl.multiple_of(j * GRID_W, GRID_W)
            qj = (q_ref[0, pl.ds(qoff, GRID_W), :] * scale).astype(BF16)
            kw = ks_ref[pl.ds(start, WIN_R * GRID_W), :]
            vw = vs_ref[pl.ds(start, WIN_R * GRID_W), :]
            outs = []
            for h in range(NA_HEADS):
                sl = slice(h * NA_HD, (h + 1) * NA_HD)
                qh = qj[:, sl]
                s = lax.dot_general(qh, kw[:, sl], _NT, preferred_element_type=F32)
                s = s + bias_ref[h, pat]
                sm = lax.dot_general(qh, km[:, sl], _NT, preferred_element_type=F32)
                m = jnp.maximum(jnp.max(s, axis=-1, keepdims=True),
                                jnp.max(sm, axis=-1, keepdims=True))
                p = jnp.exp(s - m)
                pm = jnp.exp(sm - m)
                den = jnp.sum(p, axis=-1, keepdims=True) + jnp.sum(pm, axis=-1, keepdims=True)
                oh = (jnp.dot(p.astype(BF16), vw[:, sl], preferred_element_type=F32)
                      + jnp.dot(pm.astype(BF16), vm[:, sl], preferred_element_type=F32))
                outs.append(oh / den)
            oj = jnp.concatenate(outs, axis=-1)
            o_ref[0, pl.ds(qoff, GRID_W), :] = oj * _silu(g_ref[0, pl.ds(qoff, GRID_W), :])
            return carry

        lax.fori_loop(0, NA_QROWS, row_body, 0)


def _na_bias_table(rpb):
    cols = np.arange(GRID_W)
    cstart = np.clip(cols - WIN_C // 2, 0, GRID_W - WIN_C)
    kc = np.arange(GRID_W)
    inwin = (kc[None, :] >= cstart[:, None]) & (kc[None, :] < cstart[:, None] + WIN_C)
    dc = np.clip(kc[None, :] - cols[:, None] + (WIN_C - 1), 0, 2 * WIN_C - 2)
    pats = np.arange(WIN_R)
    jj = np.arange(WIN_R)
    dr = jj[None, :] - pats[:, None] + (WIN_R - 1)
    b = rpb[:, dr][:, :, :, dc]
    b = jnp.where(inwin[None, None, None], b, NEG_BIG)
    b = b.transpose(0, 1, 3, 2, 4).reshape(NA_HEADS, WIN_R, GRID_W, WIN_R * GRID_W)
    return b.astype(F32)


def _na_branch(zna, bias, rows):
    bsz, lp, _ = zna.shape
    nb = lp // ROW_TILE
    mblk = META_ROW0 // N_META
    tile = (1, ROW_TILE, BRANCH_W)
    prev = lambda b, i: jnp.maximum(i - 1, 0)
    nxt = lambda b, i: jnp.minimum(i + 1, nb - 1)
    in_specs = [
        pl.BlockSpec(tile, lambda b, i: (b, i, 0)),
        pl.BlockSpec(tile, lambda b, i: (b, i, 3)),
        pl.BlockSpec(tile, lambda b, i: (b, prev(b, i), 1)),
        pl.BlockSpec(tile, lambda b, i: (b, i, 1)),
        pl.BlockSpec(tile, lambda b, i: (b, nxt(b, i), 1)),
        pl.BlockSpec(tile, lambda b, i: (b, prev(b, i), 2)),
        pl.BlockSpec(tile, lambda b, i: (b, i, 2)),
        pl.BlockSpec(tile, lambda b, i: (b, nxt(b, i), 2)),
        pl.BlockSpec((1, N_META, BRANCH_W), lambda b, i: (b, mblk, 1)),
        pl.BlockSpec((1, N_META, BRANCH_W), lambda b, i: (b, mblk, 2)),
        pl.BlockSpec(bias.shape, lambda b, i: (0, 0, 0, 0)),
    ]
    return pl.pallas_call(
        functools.partial(_na_kernel, rows=rows),
        grid=(bsz, nb),
        in_specs=in_specs,
        out_specs=pl.BlockSpec(tile, lambda b, i: (b, i, 0)),
        out_shape=jax.ShapeDtypeStruct((bsz, lp, BRANCH_W), F32),
        scratch_shapes=[pltpu.VMEM((3 * ROW_TILE, BRANCH_W), BF16),
                        pltpu.VMEM((3 * ROW_TILE, BRANCH_W), BF16)],
        compiler_params=_cparams(("parallel", "parallel")),
        name="na_attn",
    )(zna, zna, zna, zna, zna, zna, zna, zna, zna, zna, bias)


def _chunk_of(d, n, nchunks):
    back = jnp.where(n <= META_CHUNK, n, nchunks + META_CHUNK - n)
    return jnp.where(d == 0, n, back)


def _order_masks(rev):
    row = lax.broadcasted_iota(jnp.int32, (CHUNK, CHUNK), 0)
    col = lax.broadcasted_iota(jnp.int32, (CHUNK, CHUNK), 1)
    ahead = (col - row) * jnp.where(rev, -1, 1)
    return ahead <= 0, ahead < 0


def _gla_kernel(z_ref, gup_ref, gb_ref, o_ref, st_ref, *, nchunks):
    d = pl.program_id(0)
    n = pl.program_id(2)
    c = _chunk_of(d, n, nchunks)

    @pl.when(n == 0)
    def _init():
        st_ref[...] = jnp.zeros_like(st_ref)

    @pl.when(c < META_CHUNK)
    def _padding():
        o_ref[0, 0] = jnp.zeros((CHUNK, BRANCH_W), F32)

    @pl.when(c >= META_CHUNK)
    def _chunk():
        rev = jnp.logical_and(d == 1, c > META_CHUNK)
        incl, _ = _order_masks(rev)
        z = z_ref[0]
        q = z[:, 0:GLA_KW] * (GLA_DK ** -0.5)
        k = z[:, GLA_KW:2 * GLA_KW]
        v = z[:, 2 * GLA_KW:2 * GLA_KW + BRANCH_W]
        gd = z[:, 2 * GLA_KW + 2 * BRANCH_W:]
        logits = _mm3(gd, gup_ref[0]) + gb_ref[0]
        gk = (jnp.minimum(logits, 0.0) - jnp.log(1.0 + jnp.exp(-jnp.abs(logits)))) / GLA_NORMALIZER
        rowid = lax.broadcasted_iota(jnp.int32, (CHUNK, 1), 0)
        live = jnp.logical_or(c > META_CHUNK, rowid >= META_LOCAL0)
        gk = jnp.where(live, gk, 0.0)
        bcum = _mm_exact_lhs(incl.astype(BF16), gk)
        tot = jnp.sum(gk, axis=0, keepdims=True)
        qe = q * jnp.exp(bcum)
        ke = k * jnp.exp(-bcum)
        kd = k * jnp.exp(tot - bcum)
        st = st_ref[...]
        outs, dst = [], []
        for h in range(GLA_HEADS):
            ks = slice(h * GLA_DK, (h + 1) * GLA_DK)
            vh = v[:, h * GLA_DV:(h + 1) * GLA_DV]
            a = jnp.where(incl, _mm_nt(qe[:, ks], ke[:, ks]), 0.0)
            outs.append(_mm(a, vh) + _mm_nt(qe[:, ks], st[:, ks]))
            dst.append(_mm_tn(vh, kd[:, ks]))
        o_ref[0, 0] = jnp.concatenate(outs, axis=-1)
        st_ref[...] = st * jnp.exp(tot) + jnp.concatenate(dst, axis=-1)


def _gla_branch(zgla, gup_pad, gb):
    bsz, lp, zc = zgla.shape
    nchunks = lp // CHUNK
    cmap = lambda d, b, n: (b, _chunk_of(d, n, nchunks), 0)
    return pl.pallas_call(
        functools.partial(_gla_kernel, nchunks=nchunks),
        grid=(2, bsz, nchunks),
        in_specs=[
            pl.BlockSpec((1, CHUNK, zc), cmap),
            pl.BlockSpec((1, LANES, GLA_KW), lambda d, b, n: (d, 0, 0)),
            pl.BlockSpec((1, 1, GLA_KW), lambda d, b, n: (d, 0, 0)),
        ],
        out_specs=pl.BlockSpec((1, 1, CHUNK, BRANCH_W),
                               lambda d, b, n: (d, b, _chunk_of(d, n, nchunks), 0)),
        out_shape=jax.ShapeDtypeStruct((2, bsz, lp, BRANCH_W), F32),
        scratch_shapes=[pltpu.VMEM((GLA_DV, GLA_KW), F32)],
        compiler_params=_cparams(("parallel", "parallel", "arbitrary")),
        name="gla_scan",
    )(zgla, gup_pad, gb)


HALO = 8


def _rw_prep_kernel(*refs, mix):
    if mix:
        (z_ref, zp_ref, zn_ref, vf_ref, conv_ref, w0_ref, wup_ref, a0_ref, aup_ref, kk_ref, ka_ref,
         rk_ref, ones_ref, v0_ref, vdn_ref, vup_ref,
         r_o, k_o, v_o, kkn_o, lw_o, al_o, bonus_o, sc_ref) = refs
    else:
        (z_ref, zp_ref, zn_ref, conv_ref, w0_ref, wup_ref, a0_ref, aup_ref, kk_ref, ka_ref,
         rk_ref, ones_ref,
         r_o, k_o, v_o, kkn_o, lw_o, al_o, bonus_o, sc_ref) = refs
    i = pl.program_id(1)
    last = pl.num_programs(1) - 1
    nsh = 3 * BRANCH_W
    sc_ref[HALO:HALO + ROW_TILE, 0:nsh] = z_ref[0, :, 0:nsh]
    sc_ref[HALO:HALO + ROW_TILE, nsh:] = z_ref[0, :, nsh + BRANCH_W:]
    sc_ref[0:HALO, 0:nsh] = zp_ref[0, :, 0:nsh]
    sc_ref[0:HALO, nsh:] = zp_ref[0, :, nsh + BRANCH_W:]
    nxt_keep = jnp.where(i == last, 0.0, 1.0)
    sc_ref[HALO + ROW_TILE:, 0:nsh] = zn_ref[0, :, 0:nsh] * nxt_keep
    sc_ref[HALO + ROW_TILE:, nsh:] = zn_ref[0, :, nsh + BRANCH_W:] * nxt_keep
    zm = sc_ref[HALO - 1:HALO - 1 + ROW_TILE, :]
    z0 = sc_ref[HALO:HALO + ROW_TILE, :]
    zp = sc_ref[HALO + 1:HALO + 1 + ROW_TILE, :]
    cw = conv_ref[...]
    zs = zm * cw[0:1] + z0 * cw[1:2] + zp * cw[2:3]
    r = zs[:, 0:BRANCH_W]
    k = zs[:, BRANCH_W:2 * BRANCH_W]
    v = zs[:, 2 * BRANCH_W:nsh]
    wd = zs[:, nsh:nsh + 2 * RW_W_LORA]
    ad = zs[:, nsh + 2 * RW_W_LORA:]
    rowid = lax.broadcasted_iota(jnp.int32, (ROW_TILE, 1), 0)
    live = jnp.logical_or(i > 0, rowid >= META_ROW0)
    twd = jnp.tanh(wd)
    als = []
    for d in range(2):
        wl = _mm3(twd, wup_ref[d]) + w0_ref[d]
        lw = -jnp.exp(-_softplus(-wl) - 0.5)
        lw_o[d, 0] = jnp.where(live, lw, 0.0)
        al = _sigmoid(_mm3(ad, aup_ref[d]) + a0_ref[d])
        al_o[d, 0] = al
        als.append(al)
    kkr = k * kk_ref[...]
    nrm2 = _mm_exact_rhs(kkr * kkr, ones_ref[...])
    kkn = kkr / jnp.maximum(jnp.sqrt(nrm2), 1e-12)
    ka = ka_ref[...]
    kmod0 = k * (1.0 + (als[0] - 1.0) * ka)
    kmod1 = k * (1.0 + (als[1] - 1.0) * ka)
    if mix:
        gate = _sigmoid(v0_ref[...] + _mm3(_mm3(v, vdn_ref[...]), vup_ref[...]))
        v = v + (vf_ref[0] - v) * gate
    kb = 0.5 * (kmod0 + kmod1)
    bsum = _mm_exact_rhs(r * kb * rk_ref[...], ones_ref[...])
    bonus_o[0] = bsum * v
    r_o[0] = r
    k_o[0] = jnp.where(live, k, 0.0)
    v_o[0] = jnp.where(live, v, 0.0)
    kkn_o[0] = jnp.where(live, kkn, 0.0)


def _rw_prep(zrw, vfirst, conv_w, w0, wup_pad, a0, aup_pad, k_k, k_a, r_k, ones_bd, vmix):
    bsz, lp, zc = zrw.shape
    nb = lp // ROW_TILE
    nhalo = lp // HALO
    per_tile = ROW_TILE // HALO
    mix = vmix is not None
    tile = pl.BlockSpec((1, ROW_TILE, BRANCH_W), lambda b, i: (b, i, 0))
    dtile = pl.BlockSpec((2, 1, ROW_TILE, BRANCH_W), lambda b, i: (0, b, i, 0))
    full = lambda a: pl.BlockSpec(a.shape, lambda b, i: (0,) * a.ndim)
    in_specs = [
        pl.BlockSpec((1, ROW_TILE, zc), lambda b, i: (b, i, 0)),
        pl.BlockSpec((1, HALO, zc), lambda b, i: (b, jnp.maximum(i * per_tile - 1, 0), 0)),
        pl.BlockSpec((1, HALO, zc), lambda b, i: (b, jnp.minimum((i + 1) * per_tile, nhalo - 1), 0)),
    ]
    args = [zrw, zrw, zrw]
    if mix:
        in_specs.append(tile)
        args.append(vfirst)
    params = [conv_w, w0, wup_pad, a0, aup_pad, k_k, k_a, r_k, ones_bd]
    if mix:
        params += list(vmix)
    in_specs += [full(p) for p in params]
    args += params
    sd = jax.ShapeDtypeStruct((bsz, lp, BRANCH_W), F32)
    sd2 = jax.ShapeDtypeStruct((2, bsz, lp, BRANCH_W), F32)
    return pl.pallas_call(
        functools.partial(_rw_prep_kernel, mix=mix),
        grid=(bsz, nb),
        in_specs=in_specs,
        out_specs=[tile, tile, tile, tile, dtile, dtile, tile],
        out_shape=[sd, sd, sd, sd, sd2, sd2, sd],
        scratch_shapes=[pltpu.VMEM((ROW_TILE + 2 * HALO, 3 * BRANCH_W + RW_LORA_COLS), F32)],
        compiler_params=_cparams(("parallel", "parallel")),
        name="rw_prep",
    )(*args)


def _rw_scan_kernel(r_ref, k_ref, v_ref, kk_ref, lw_ref, al_ref, ka_ref, y_ref, ht_ref, *, nchunks):
    d = pl.program_id(0)
    n = pl.program_id(2)
    c = _chunk_of(d, n, nchunks)

    @pl.when(n == 0)
    def _init():
        ht_ref[...] = jnp.zeros_like(ht_ref)

    @pl.when(c < META_CHUNK)
    def _padding():
        y_ref[0, 0] = jnp.zeros((CHUNK, BRANCH_W), F32)

    @pl.when(c >= META_CHUNK)
    def _chunk():
        rev = jnp.logical_and(d == 1, c > META_CHUNK)
        incl_m, strict_m = _order_masks(rev)
        lw = lw_ref[0, 0]
        al = al_ref[0, 0]
        r = r_ref[0]
        k = k_ref[0]
        v = v_ref[0]
        kk = kk_ref[0]
        incl = _mm_exact_lhs(incl_m.astype(BF16), lw)
        tot = jnp.sum(lw, axis=0, keepdims=True)
        e_in = jnp.exp(incl)
        e_inv = jnp.exp(-incl)
        e_end = jnp.exp(tot - incl)
        kmod = k * (1.0 + (al - 1.0) * ka_ref[...])
        bvec = kk * al
        at = -kk * jnp.exp(incl - lw)
        rt = r * e_in
        bt = bvec * e_inv
        kt = kmod * e_inv
        bp = bvec * e_end
        kp = kmod * e_end
        ht = ht_ref[...]
        ys, dhs = [], []
        for h in range(RW_HEADS):
            sl = slice(h * RW_N, (h + 1) * RW_N)
            vh = v[:, sl]
            a_ab = jnp.where(strict_m, _mm_nt(at[:, sl], bt[:, sl]), 0.0)
            a_ak = jnp.where(strict_m, _mm_nt(at[:, sl], kt[:, sl]), 0.0)
            a_rb = jnp.where(incl_m, _mm_nt(rt[:, sl], bt[:, sl]), 0.0)
            a_rk = jnp.where(incl_m, _mm_nt(rt[:, sl], kt[:, sl]), 0.0)
            x = jnp.concatenate([at[:, sl], _mm(a_ak, vh)], axis=-1)
            nmat = a_ab
            steps = int(math.log2(CHUNK))
            for s in range(steps):
                if s < steps - 1:
                    zz = _mm(nmat, jnp.concatenate([x, nmat], axis=-1))
                    x = x + zz[:, :2 * RW_N]
                    nmat = zz[:, 2 * RW_N:]
                else:
                    x = x + _mm(nmat, x)
            w = x[:, :RW_N]
            u0 = x[:, RW_N:]
            hth = ht[:, sl]
            u = _mm_nt(w, hth) + u0
            ys.append(_mm_nt(rt[:, sl], hth) + _mm(a_rb, u) + _mm(a_rk, vh))
            dhs.append(_mm_tn(u, bp[:, sl]) + _mm_tn(vh, kp[:, sl]))
        y_ref[0, 0] = jnp.concatenate(ys, axis=-1)
        ht_ref[...] = ht * jnp.exp(tot) + jnp.concatenate(dhs, axis=-1)


def _rw_scan(r, k, v, kk, lw, al, k_a):
    bsz, lp, _ = r.shape
    nchunks = lp // CHUNK
    cmap = lambda d, b, n: (b, _chunk_of(d, n, nchunks), 0)
    dmap = lambda d, b, n: (d, b, _chunk_of(d, n, nchunks), 0)
    tok = pl.BlockSpec((1, CHUNK, BRANCH_W), cmap)
    dtok = pl.BlockSpec((1, 1, CHUNK, BRANCH_W), dmap)
    return pl.pallas_call(
        functools.partial(_rw_scan_kernel, nchunks=nchunks),
        grid=(2, bsz, nchunks),
        in_specs=[tok, tok, tok, tok, dtok, dtok,
                  pl.BlockSpec((1, BRANCH_W), lambda d, b, n: (0, 0))],
        out_specs=dtok,
        out_shape=jax.ShapeDtypeStruct((2, bsz, lp, BRANCH_W), F32),
        scratch_shapes=[pltpu.VMEM((RW_N, BRANCH_W), F32)],
        compiler_params=_cparams(("parallel", "parallel", "arbitrary")),
        name="rw_scan",
    )(r, k, v, kk, lw, al, k_a)


def _out_kernel(h_ref, ona_ref, og_ref, gg_ref, y_ref, bonus_ref, gate_ref, wout_ref, gng_ref,
                avg_ref, lng_ref, lnb_ref, fin_ref, o_ref, *, final, tile_off):
    i = pl.program_id(1) + tile_off
    og = og_ref[0, 0] + og_ref[1, 0]
    parts = []
    for h in range(GLA_HEADS):
        oh = og[:, h * GLA_DV:(h + 1) * GLA_DV]
        ms = jnp.mean(oh * oh, axis=-1, keepdims=True)
        parts.append(oh * lax.rsqrt(ms + GLA_NORM_EPS) * gng_ref[...])
    o_gla = jnp.concatenate(parts, axis=-1) * _silu(gg_ref[0])
    y = y_ref[0, 0] + y_ref[1, 0]
    mu = _mm_exact_rhs(y, avg_ref[...])
    yc = y - mu
    var = _mm_exact_rhs(yc * yc, avg_ref[...])
    yn = yc * lax.rsqrt(var + RW_GN_EPS) * lng_ref[...] + lnb_ref[...]
    o_rw = (yn + bonus_ref[0]) * _silu(gate_ref[0])
    cat = jnp.concatenate([ona_ref[0].astype(BF16), o_gla.astype(BF16), o_rw.astype(BF16)], axis=-1)
    hn = h_ref[0] + jnp.dot(cat, wout_ref[...], preferred_element_type=F32)
    if final:
        ms = jnp.mean(hn * hn, axis=-1, keepdims=True)
        o_ref[0] = hn * lax.rsqrt(ms + NORM_EPS) * fin_ref[...]
    else:
        rowid = lax.broadcasted_iota(jnp.int32, (ROW_TILE, 1), 0)
        live = jnp.logical_or(i > 0, rowid >= META_ROW0)
        o_ref[0] = jnp.where(live, hn, 0.0)


def _out_proj(hp, ona, ogla, zgla, yrw, bonus, zrw, wout, gla_norm_g, avg_bd, ln_g, ln_b, fin_g, final):
    bsz, lp, _ = hp.shape
    nb = lp // ROW_TILE
    off = 1 if final else 0
    nt = nb - off
    tile = lambda w, cb: pl.BlockSpec((1, ROW_TILE, w), lambda b, i: (b, i + off, cb))
    dtile = pl.BlockSpec((2, 1, ROW_TILE, BRANCH_W), lambda b, i: (0, b, i + off, 0))
    full = lambda a: pl.BlockSpec(a.shape, lambda b, i: (0,) * a.ndim)
    gla_gate_blk = (2 * GLA_KW + BRANCH_W) // BRANCH_W
    rw_gate_blk = 3
    out_rows = lp - off * ROW_TILE
    return pl.pallas_call(
        functools.partial(_out_kernel, final=final, tile_off=off),
        grid=(bsz, nt),
        in_specs=[
            tile(D_MODEL, 0), tile(BRANCH_W, 0), dtile, tile(BRANCH_W, gla_gate_blk),
            dtile, tile(BRANCH_W, 0), tile(BRANCH_W, rw_gate_blk),
            full(wout), full(gla_norm_g), full(avg_bd), full(ln_g), full(ln_b), full(fin_g),
        ],
        out_specs=pl.BlockSpec((1, ROW_TILE, D_MODEL), lambda b, i: (b, i, 0)),
        out_shape=jax.ShapeDtypeStruct((bsz, out_rows, D_MODEL), F32),
        compiler_params=_cparams(("parallel", "parallel")),
        name="out_proj",
    )(hp, ona, ogla, zgla, yrw, bonus, zrw, wout, gla_norm_g, avg_bd, ln_g, ln_b, fin_g)


def _block_diag(n, blk, val):
    idx = np.arange(n) // blk
    return jnp.asarray((idx[:, None] == idx[None, :]).astype(np.float32) * val)


def _pad_rows_per_dir(w, rows_total):
    _, r, c = w.shape
    out = jnp.zeros((2, rows_total, c), w.dtype)
    out = out.at[0, 0:r].set(w[0])
    out = out.at[1, r:2 * r].set(w[1])
    return out


def kernel(x, meta, norm_g, w_in, w_out, na_rpb, gla_g_up, gla_g_b, gla_norm_g, rw_conv, rw_w0, rw_w_up,
           rw_a0, rw_a_up, rw_k_k, rw_k_a, rw_r_k, rw_ln_g, rw_ln_b, rw_v0, rw_v_down, rw_v_up, final_norm_g):
    bsz, t, dm = x.shape
    assert dm == D_MODEL and t % ROW_TILE == 0 and t // GRID_W >= WIN_R
    depth = w_in.shape[0]
    lp = t + PAD
    rows = t // GRID_W
    hp = jnp.concatenate([
        jnp.zeros((bsz, META_ROW0, dm), x.dtype),
        jnp.broadcast_to(meta[None].astype(x.dtype), (bsz, N_META, dm)),
        x], axis=1)
    ones_bd = _block_diag(BRANCH_W, RW_N, 1.0).astype(BF16)
    avg_bd = _block_diag(BRANCH_W, RW_N, 1.0 / RW_N).astype(BF16)
    nsh = 3 * BRANCH_W
    vfirst = None
    out = None
    for l in range(depth):
        w = w_in[l]
        wna = w[:, :NA_COLS].astype(BF16)
        wg = w[:, NA_COLS:NA_COLS + GLA_COLS]
        wgla = jnp.pad(wg, ((0, 0), (0, GLA_ZCOLS - GLA_COLS))).astype(BF16)
        wr = w[:, NA_COLS + GLA_COLS:]
        wrw = jnp.concatenate([wr[:, :nsh], wr[:, RW_SHIFT_COLS:], wr[:, nsh:RW_SHIFT_COLS]], axis=1).astype(BF16)
        conv_w = rw_conv[l]
        zna, zgla, zrw = _in_proj(hp.reshape(bsz * lp, dm), norm_g[l][None], wna, wgla, wrw)
        zna = zna.reshape(bsz, lp, -1)
        zgla = zgla.reshape(bsz, lp, -1)
        zrw = zrw.reshape(bsz, lp, -1)

        ona = _na_branch(zna, _na_bias_table(na_rpb[l]), rows)

        gup_pad = _pad_rows_per_dir(gla_g_up[l], LANES)
        ogla = _gla_branch(zgla, gup_pad, gla_g_b[l][:, None, :])

        wup_pad = _pad_rows_per_dir(rw_w_up[l], 2 * RW_W_LORA)
        aup_pad = _pad_rows_per_dir(rw_a_up[l], 2 * RW_A_LORA)
        if l == 0:
            vmix = None
        else:
            vdn = jnp.pad(rw_v_down[l - 1], ((0, 0), (0, LANES - RW_V_LORA)))
            vup = jnp.pad(rw_v_up[l - 1], ((0, LANES - RW_V_LORA), (0, 0)))
            vmix = (rw_v0[l - 1][None], vdn, vup)
        r_, k_, v_, kk_, lw_, al_, bonus = _rw_prep(
            zrw, vfirst, conv_w, rw_w0[l][:, None, :], wup_pad, rw_a0[l][:, None, :], aup_pad,
            rw_k_k[l][None], rw_k_a[l][None], rw_r_k[l][None], ones_bd, vmix)
        if l == 0:
            vfirst = v_
        yrw = _rw_scan(r_, k_, v_, kk_, lw_, al_, rw_k_a[l][None])

        final = l == depth - 1
        res = _out_proj(hp, ona, ogla, zgla, yrw, bonus, zrw, w_out[l].astype(BF16), gla_norm_g[l][None],
                        avg_bd, rw_ln_g[l][None], rw_ln_b[l][None], final_norm_g[None], final)
        if final:
            out = res
        else:
            hp = res
    return out
```

```python
import functools
import math

import numpy as np
import jax
import jax.numpy as jnp
from jax import lax
from jax.experimental import pallas as pl
from jax.experimental.pallas import tpu as pltpu

F32 = jnp.float32
BF16 = jnp.bfloat16

D_MODEL = 1024
N_META = 16
GRID_W = 64
BRANCH_W = D_MODEL // 2
D_MIX = 3 * BRANCH_W
NORM_EPS = 1e-6
NA_HD = 64
NA_HEADS = BRANCH_W // NA_HD
WIN_R = 8
WIN_C = 16
GLA_HEADS = 4
GLA_DV = BRANCH_W // GLA_HEADS
GLA_DK = GLA_DV // 2
GLA_KW = GLA_HEADS * GLA_DK
GLA_LR = 16
GLA_NORMALIZER = 16.0
GLA_NORM_EPS = 1e-5
RW_N = 64
RW_HEADS = BRANCH_W // RW_N
RW_W_LORA = 64
RW_A_LORA = 64
RW_V_LORA = 32
RW_GN_EPS = 64e-5
NA_COLS = 4 * BRANCH_W
GLA_COLS = 2 * GLA_KW + 2 * BRANCH_W + 2 * GLA_LR
RW_SHIFT_COLS = 3 * BRANCH_W + 2 * RW_W_LORA + 2 * RW_A_LORA
RW_COLS = RW_SHIFT_COLS + BRANCH_W

LANES = 128
ROW_TILE = 512
CHUNK = 64
PAD = ROW_TILE
META_ROW0 = PAD - N_META
META_CHUNK = META_ROW0 // CHUNK
META_LOCAL0 = META_ROW0 - META_CHUNK * CHUNK
GLA_ZCOLS = 2 * GLA_KW + 2 * BRANCH_W + LANES
RW_LORA_COLS = 2 * RW_W_LORA + 2 * RW_A_LORA
NEG_BIG = -1e30
VMEM_LIMIT = 56 * 1024 * 1024

_NT = (((1,), (1,)), ((), ()))
_TN = (((0,), (0,)), ((), ()))


def _mm(a, b):
    return jnp.dot(a.astype(BF16), b.astype(BF16), preferred_element_type=F32)


def _mm_nt(a, b):
    return lax.dot_general(a.astype(BF16), b.astype(BF16), _NT, preferred_element_type=F32)


def _mm_tn(a, b):
    return lax.dot_general(a.astype(BF16), b.astype(BF16), _TN, preferred_element_type=F32)


def _split2(x):
    hi = x.astype(BF16)
    lo = (x - hi.astype(F32)).astype(BF16)
    return hi, lo


def _split3(x):
    hi = x.astype(BF16)
    r = x - hi.astype(F32)
    mid = r.astype(BF16)
    lo = (r - mid.astype(F32)).astype(BF16)
    return hi, mid, lo


def _mm3(a, b):
    ah, al = _split2(a)
    bh, bl = _split2(b)
    d = functools.partial(jnp.dot, preferred_element_type=F32)
    return d(ah, bh) + (d(ah, bl) + d(al, bh))


def _mm_exact_lhs(a_bf16, x):
    xh, xm, xl = _split3(x)
    d = functools.partial(jnp.dot, preferred_element_type=F32)
    return d(a_bf16, xh) + (d(a_bf16, xm) + d(a_bf16, xl))


def _mm_exact_rhs(x, b_bf16):
    xh, xm, xl = _split3(x)
    d = functools.partial(jnp.dot, preferred_element_type=F32)
    return d(xh, b_bf16) + (d(xm, b_bf16) + d(xl, b_bf16))


def _sigmoid(x):
    return 1.0 / (1.0 + jnp.exp(-x))


def _silu(x):
    return x * _sigmoid(x)


def _softplus(x):
    return jnp.maximum(x, 0.0) + jnp.log(1.0 + jnp.exp(-jnp.abs(x)))


def _cparams(sem):
    return pltpu.CompilerParams(dimension_semantics=sem, vmem_limit_bytes=VMEM_LIMIT)


def _inproj_kernel(h_ref, g_ref, wna_ref, wgla_ref, wrw_ref, zna_ref, zgla_ref, zrw_ref):
    h = h_ref[...]
    ms = jnp.mean(h * h, axis=-1, keepdims=True)
    hn = (h * lax.rsqrt(ms + NORM_EPS) * g_ref[...]).astype(BF16)
    zna_ref[...] = jnp.dot(hn, wna_ref[...], preferred_element_type=F32)
    zgla_ref[...] = jnp.dot(hn, wgla_ref[...], preferred_element_type=F32)
    zrw_ref[...] = jnp.dot(hn, wrw_ref[...], preferred_element_type=F32)


def _in_proj(h2d, norm_g, wna, wgla, wrw):
    n = h2d.shape[0]
    tm = ROW_TILE // 2
    const = lambda i: (0, 0)
    row = lambda i: (i, 0)
    return pl.pallas_call(
        _inproj_kernel,
        grid=(n // tm,),
        in_specs=[
            pl.BlockSpec((tm, D_MODEL), row),
            pl.BlockSpec((1, D_MODEL), const),
            pl.BlockSpec(wna.shape, const),
            pl.BlockSpec(wgla.shape, const),
            pl.BlockSpec(wrw.shape, const),
        ],
        out_specs=[
            pl.BlockSpec((tm, wna.shape[1]), row),
            pl.BlockSpec((tm, wgla.shape[1]), row),
            pl.BlockSpec((tm, wrw.shape[1]), row),
        ],
        out_shape=[
            jax.ShapeDtypeStruct((n, wna.shape[1]), F32),
            jax.ShapeDtypeStruct((n, wgla.shape[1]), F32),
            jax.ShapeDtypeStruct((n, wrw.shape[1]), F32),
        ],
        compiler_params=_cparams(("parallel",)),
        name="in_proj",
    )(h2d, norm_g, wna, wgla, wrw)


NA_QROWS = ROW_TILE // GRID_W


def _na_kernel(q_ref, g_ref, kp_ref, kc_ref, kn_ref, vp_ref, vc_ref, vn_ref, km_ref, vm_ref,
               bias_ref, o_ref, ks_ref, vs_ref, *, rows):
    i = pl.program_id(1)
    scale = NA_HD ** -0.5
    km = km_ref[0].astype(BF16)
    vm = vm_ref[0].astype(BF16)

    @pl.when(i == 0)
    def _meta_tile():
        o_ref[0, 0:META_ROW0, :] = jnp.zeros((META_ROW0, BRANCH_W), F32)
        qm = (q_ref[0, META_ROW0:PAD, :] * scale).astype(BF16)
        outs = []
        for h in range(NA_HEADS):
            sl = slice(h * NA_HD, (h + 1) * NA_HD)
            s = lax.dot_general(qm[:, sl], km[:, sl], _NT, preferred_element_type=F32)
            m = jnp.max(s, axis=-1, keepdims=True)
            p = jnp.exp(s - m)
            den = jnp.sum(p, axis=-1, keepdims=True)
            outs.append(jnp.dot(p.astype(BF16), vm[:, sl], preferred_element_type=F32) / den)
        om = jnp.concatenate(outs, axis=-1)
        o_ref[0, META_ROW0:PAD, :] = om * _silu(g_ref[0, META_ROW0:PAD, :])

    @pl.when(i > 0)
    def _real_tile():
        ks_ref[0:ROW_TILE, :] = kp_ref[0].astype(BF16)
        ks_ref[ROW_TILE:2 * ROW_TILE, :] = kc_ref[0].astype(BF16)
        ks_ref[2 * ROW_TILE:3 * ROW_TILE, :] = kn_ref[0].astype(BF16)
        vs_ref[0:ROW_TILE, :] = vp_ref[0].astype(BF16)
        vs_ref[ROW_TILE:2 * ROW_TILE, :] = vc_ref[0].astype(BF16)
        vs_ref[2 * ROW_TILE:3 * ROW_TILE, :] = vn_ref[0].astype(BF16)
        r0 = (i - 1) * NA_QROWS

        def row_body(j, carry):
            r = r0 + j
            rs = jnp.clip(r - WIN_R // 2, 0, rows - WIN_R)
            start = pl.multiple_of((rs - r0 + NA_QROWS) * GRID_W, GRID_W)
            pat = r - rs
            qoff = pl.multiple_of(j * GRID_W, GRID_W)
            qj = (q_ref[0, pl.ds(qoff, GRID_W), :] * scale).astype(BF16)
            kw = ks_ref[pl.ds(start, WIN_R * GRID_W), :]
            vw = vs_ref[pl.ds(start, WIN_R * GRID_W), :]
            outs = []
            for h in range(NA_HEADS):
                sl = slice(h * NA_HD, (h + 1) * NA_HD)
                qh = qj[:, sl]
                s = lax.dot_general(qh, kw[:, sl], _NT, preferred_element_type=F32)
                s = s + bias_ref[h, pat]
                sm = lax.dot_general(qh, km[:, sl], _NT, preferred_element_type=F32)
                m = jnp.maximum(jnp.max(s, axis=-1, keepdims=True),
                                jnp.max(sm, axis=-1, keepdims=True))
                p = jnp.exp(s - m)
                pm = jnp.exp(sm - m)
                den = jnp.sum(p, axis=-1, keepdims=True) + jnp.sum(pm, axis=-1, keepdims=True)
                oh = (jnp.dot(p.astype(BF16), vw[:, sl], preferred_element_type=F32)
                      + jnp.dot(pm.astype(BF16), vm[:, sl], preferred_element_type=F32))
                outs.append(oh / den)
            oj = jnp.concatenate(outs, axis=-1)
            o_ref[0, pl.ds(qoff, GRID_W), :] = oj * _silu(g_ref[0, pl.ds(qoff, GRID_W), :])
            return carry

        lax.fori_loop(0, NA_QROWS, row_body, 0)


def _na_bias_table(rpb):
    cols = np.arange(GRID_W)
    cstart = np.clip(cols - WIN_C // 2, 0, GRID_W - WIN_C)
    kc = np.arange(GRID_W)
    inwin = (kc[None, :] >= cstart[:, None]) & (kc[None, :] < cstart[:, None] + WIN_C)
    dc = np.clip(kc[None, :] - cols[:, None] + (WIN_C - 1), 0, 2 * WIN_C - 2)
    pats = np.arange(WIN_R)
    jj = np.arange(WIN_R)
    dr = jj[None, :] - pats[:, None] + (WIN_R - 1)
    b = rpb[:, dr][:, :, :, dc]
    b = jnp.where(inwin[None, None, None], b, NEG_BIG)
    b = b.transpose(0, 1, 3, 2, 4).reshape(NA_HEADS, WIN_R, GRID_W, WIN_R * GRID_W)
    return b.astype(F32)


def _na_branch(zna, bias, rows):
    bsz, lp, _ = zna.shape
    nb = lp // ROW_TILE
    mblk = META_ROW0 // N_META
    tile = (1, ROW_TILE, BRANCH_W)
    prev = lambda b, i: jnp.maximum(i - 1, 0)
    nxt = lambda b, i: jnp.minimum(i + 1, nb - 1)
    in_specs = [
        pl.BlockSpec(tile, lambda b, i: (b, i, 0)),
        pl.BlockSpec(tile, lambda b, i: (b, i, 3)),
        pl.BlockSpec(tile, lambda b, i: (b, prev(b, i), 1)),
        pl.BlockSpec(tile, lambda b, i: (b, i, 1)),
        pl.BlockSpec(tile, lambda b, i: (b, nxt(b, i), 1)),
        pl.BlockSpec(tile, lambda b, i: (b, prev(b, i), 2)),
        pl.BlockSpec(tile, lambda b, i: (b, i, 2)),
        pl.BlockSpec(tile, lambda b, i: (b, nxt(b, i), 2)),
        pl.BlockSpec((1, N_META, BRANCH_W), lambda b, i: (b, mblk, 1)),
        pl.BlockSpec((1, N_META, BRANCH_W), lambda b, i: (b, mblk, 2)),
        pl.BlockSpec(bias.shape, lambda b, i: (0, 0, 0, 0)),
    ]
    return pl.pallas_call(
        functools.partial(_na_kernel, rows=rows),
        grid=(bsz, nb),
        in_specs=in_specs,
        out_specs=pl.BlockSpec(tile, lambda b, i: (b, i, 0)),
        out_shape=jax.ShapeDtypeStruct((bsz, lp, BRANCH_W), F32),
        scratch_shapes=[pltpu.VMEM((3 * ROW_TILE, BRANCH_W), BF16),
                        pltpu.VMEM((3 * ROW_TILE, BRANCH_W), BF16)],
        compiler_params=_cparams(("parallel", "parallel")),
        name="na_attn",
    )(zna, zna, zna, zna, zna, zna, zna, zna, zna, zna, bias)


def _chunk_of(d, n, nchunks):
    back = jnp.where(n <= META_CHUNK, n, nchunks + META_CHUNK - n)
    return jnp.where(d == 0, n, back)


def _order_masks(rev):
    row = lax.broadcasted_iota(jnp.int32, (CHUNK, CHUNK), 0)
    col = lax.broadcasted_iota(jnp.int32, (CHUNK, CHUNK), 1)
    ahead = (col - row) * jnp.where(rev, -1, 1)
    return ahead <= 0, ahead < 0


def _gla_kernel(z_ref, gup_ref, gb_ref, o_ref, st_ref, *, nchunks):
    d = pl.program_id(0)
    n = pl.program_id(2)
    c = _chunk_of(d, n, nchunks)

    @pl.when(n == 0)
    def _init():
        st_ref[...] = jnp.zeros_like(st_ref)

    @pl.when(c < META_CHUNK)
    def _padding():
        o_ref[0, 0] = jnp.zeros((CHUNK, BRANCH_W), F32)

    @pl.when(c >= META_CHUNK)
    def _chunk():
        rev = jnp.logical_and(d == 1, c > META_CHUNK)
        incl, _ = _order_masks(rev)
        z = z_ref[0]
        q = z[:, 0:GLA_KW] * (GLA_DK ** -0.5)
        k = z[:, GLA_KW:2 * GLA_KW]
        v = z[:, 2 * GLA_KW:2 * GLA_KW + BRANCH_W]
        gd = z[:, 2 * GLA_KW + 2 * BRANCH_W:]
        logits = _mm3(gd, gup_ref[0]) + gb_ref[0]
        gk = (jnp.minimum(logits, 0.0) - jnp.log(1.0 + jnp.exp(-jnp.abs(logits)))) / GLA_NORMALIZER
        rowid = lax.broadcasted_iota(jnp.int32, (CHUNK, 1), 0)
        live = jnp.logical_or(c > META_CHUNK, rowid >= META_LOCAL0)
        gk = jnp.where(live, gk, 0.0)
        bcum = _mm_exact_lhs(incl.astype(BF16), gk)
        tot = jnp.sum(gk, axis=0, keepdims=True)
        qe = q * jnp.exp(bcum)
        ke = k * jnp.exp(-bcum)
        kd = k * jnp.exp(tot - bcum)
        st = st_ref[...]
        outs, dst = [], []
        for h in range(GLA_HEADS):
            ks = slice(h * GLA_DK, (h + 1) * GLA_DK)
            vh = v[:, h * GLA_DV:(h + 1) * GLA_DV]
            a = jnp.where(incl, _mm_nt(qe[:, ks], ke[:, ks]), 0.0)
            outs.append(_mm(a, vh) + _mm_nt(qe[:, ks], st[:, ks]))
            dst.append(_mm_tn(vh, kd[:, ks]))
        o_ref[0, 0] = jnp.concatenate(outs, axis=-1)
        st_ref[...] = st * jnp.exp(tot) + jnp.concatenate(dst, axis=-1)


def _gla_branch(zgla, gup_pad, gb):
    bsz, lp, zc = zgla.shape
    nchunks = lp // CHUNK
    cmap = lambda d, b, n: (b, _chunk_of(d, n, nchunks), 0)
    return pl.pallas_call(
        functools.partial(_gla_kernel, nchunks=nchunks),
        grid=(2, bsz, nchunks),
        in_specs=[
            pl.BlockSpec((1, CHUNK, zc), cmap),
            pl.BlockSpec((1, LANES, GLA_KW), lambda d, b, n: (d, 0, 0)),
            pl.BlockSpec((1, 1, GLA_KW), lambda d, b, n: (d, 0, 0)),
        ],
        out_specs=pl.BlockSpec((1, 1, CHUNK, BRANCH_W),
                               lambda d, b, n: (d, b, _chunk_of(d, n, nchunks), 0)),
        out_shape=jax.ShapeDtypeStruct((2, bsz, lp, BRANCH_W), F32),
        scratch_shapes=[pltpu.VMEM((GLA_DV, GLA_KW), F32)],
        compiler_params=_cparams(("parallel", "parallel", "arbitrary")),
        name="gla_scan",
    )(zgla, gup_pad, gb)


HALO = 8


def _rw_prep_kernel(*refs, mix):
    if mix:
        (z_ref, zp_ref, zn_ref, vf_ref, conv_ref, w0_ref, wup_ref, a0_ref, aup_ref, kk_ref, ka_ref,
         rk_ref, ones_ref, v0_ref, vdn_ref, vup_ref,
         r_o, k_o, v_o, kkn_o, lw_o, al_o, bonus_o, sc_ref) = refs
    else:
        (z_ref, zp_ref, zn_ref, conv_ref, w0_ref, wup_ref, a0_ref, aup_ref, kk_ref, ka_ref,
         rk_ref, ones_ref,
         r_o, k_o, v_o, kkn_o, lw_o, al_o, bonus_o, sc_ref) = refs
    i = pl.program_id(1)
    last = pl.num_programs(1) - 1
    nsh = 3 * BRANCH_W
    sc_ref[HALO:HALO + ROW_TILE, 0:nsh] = z_ref[0, :, 0:nsh]
    sc_ref[HALO:HALO + ROW_TILE, nsh:] = z_ref[0, :, nsh + BRANCH_W:]
    sc_ref[0:HALO, 0:nsh] = zp_ref[0, :, 0:nsh]
    sc_ref[0:HALO, nsh:] = zp_ref[0, :, nsh + BRANCH_W:]
    nxt_keep = jnp.where(i == last, 0.0, 1.0)
    sc_ref[HALO + ROW_TILE:, 0:nsh] = zn_ref[0, :, 0:nsh] * nxt_keep
    sc_ref[HALO + ROW_TILE:, nsh:] = zn_ref[0, :, nsh + BRANCH_W:] * nxt_keep
    zm = sc_ref[HALO - 1:HALO - 1 + ROW_TILE, :]
    z0 = sc_ref[HALO:HALO + ROW_TILE, :]
    zp = sc_ref[HALO + 1:HALO + 1 + ROW_TILE, :]
    cw = conv_ref[...]
    zs = zm * cw[0:1] + z0 * cw[1:2] + zp * cw[2:3]
    r = zs[:, 0:BRANCH_W]
    k = zs[:, BRANCH_W:2 * BRANCH_W]
    v = zs[:, 2 * BRANCH_W:nsh]
    wd = zs[:, nsh:nsh + 2 * RW_W_LORA]
    ad = zs[:, nsh + 2 * RW_W_LORA:]
    rowid = lax.broadcasted_iota(jnp.int32, (ROW_TILE, 1), 0)
    live = jnp.logical_or(i > 0, rowid >= META_ROW0)
    twd = jnp.tanh(wd)
    als = []
    for d in range(2):
        wl = _mm3(twd, wup_ref[d]) + w0_ref[d]
        lw = -jnp.exp(-_softplus(-wl) - 0.5)
        lw_o[d, 0] = jnp.where(live, lw, 0.0)
        al = _sigmoid(_mm3(ad, aup_ref[d]) + a0_ref[d])
        al_o[d, 0] = al
        als.append(al)
    kkr = k * kk_ref[...]
    nrm2 = _mm_exact_rhs(kkr * kkr, ones_ref[...])
    kkn = kkr / jnp.maximum(jnp.sqrt(nrm2), 1e-12)
    ka = ka_ref[...]
    kmod0 = k * (1.0 + (als[0] - 1.0) * ka)
    kmod1 = k * (1.0 + (als[1] - 1.0) * ka)
    if mix:
        gate = _sigmoid(v0_ref[...] + _mm3(_mm3(v, vdn_ref[...]), vup_ref[...]))
        v = v + (vf_ref[0] - v) * gate
    kb = 0.5 * (kmod0 + kmod1)
    bsum = _mm_exact_rhs(r * kb * rk_ref[...], ones_ref[...])
    bonus_o[0] = bsum * v
    r_o[0] = r
    k_o[0] = jnp.where(live, k, 0.0)
    v_o[0] = jnp.where(live, v, 0.0)
    kkn_o[0] = jnp.where(live, kkn, 0.0)


def _rw_prep(zrw, vfirst, conv_w, w0, wup_pad, a0, aup_pad, k_k, k_a, r_k, ones_bd, vmix):
    bsz, lp, zc = zrw.shape
    nb = lp // ROW_TILE
    nhalo = lp // HALO
    per_tile = ROW_TILE // HALO
    mix = vmix is not None
    tile = pl.BlockSpec((1, ROW_TILE, BRANCH_W), lambda b, i: (b, i, 0))
    dtile = pl.BlockSpec((2, 1, ROW_TILE, BRANCH_W), lambda b, i: (0, b, i, 0))
    full = lambda a: pl.BlockSpec(a.shape, lambda b, i: (0,) * a.ndim)
    in_specs = [
        pl.BlockSpec((1, ROW_TILE, zc), lambda b, i: (b, i, 0)),
        pl.BlockSpec((1, HALO, zc), lambda b, i: (b, jnp.maximum(i * per_tile - 1, 0), 0)),
        pl.BlockSpec((1, HALO, zc), lambda b, i: (b, jnp.minimum((i + 1) * per_tile, nhalo - 1), 0)),
    ]
    args = [zrw, zrw, zrw]
    if mix:
        in_specs.append(tile)
        args.append(vfirst)
    params = [conv_w, w0, wup_pad, a0, aup_pad, k_k, k_a, r_k, ones_bd]
    if mix:
        params += list(vmix)
    in_specs += [full(p) for p in params]
    args += params
    sd = jax.ShapeDtypeStruct((bsz, lp, BRANCH_W), F32)
    sd2 = jax.ShapeDtypeStruct((2, bsz, lp, BRANCH_W), F32)
    return pl.pallas_call(
        functools.partial(_rw_prep_kernel, mix=mix),
        grid=(bsz, nb),
        in_specs=in_specs,
        out_specs=[tile, tile, tile, tile, dtile, dtile, tile],
        out_shape=[sd, sd, sd, sd, sd2, sd2, sd],
        scratch_shapes=[pltpu.VMEM((ROW_TILE + 2 * HALO, 3 * BRANCH_W + RW_LORA_COLS), F32)],
        compiler_params=_cparams(("parallel", "parallel")),
        name="rw_prep",
    )(*args)


RW_PAIRS = RW_HEADS // 2


def _pair_block_diag(x, even):
    z = jnp.zeros_like(x)
    return jnp.concatenate([jnp.where(even, x, z), jnp.where(even, z, x)], axis=0)


def _rw_scan_kernel(rf_ref, kf_ref, vf_ref, kkf_ref, lwf_ref, alf_ref,
                    rb_ref, kb_ref, vb_ref, kkb_ref, lwb_ref, alb_ref, ka_ref,
                    yf_ref, yb_ref, ht_ref, *, nchunks):
    n = pl.program_id(1)

    @pl.when(n == 0)
    def _init():
        ht_ref[...] = jnp.zeros_like(ht_ref)

    @pl.when(n < META_CHUNK)
    def _padding():
        yf_ref[0] = jnp.zeros((CHUNK, BRANCH_W), F32)
        yb_ref[0] = jnp.zeros((CHUNK, BRANCH_W), F32)

    @pl.when(n >= META_CHUNK)
    def _chunk():
        lane = lax.broadcasted_iota(jnp.int32, (1, LANES), 1)
        even = lane < RW_N
        row = lax.broadcasted_iota(jnp.int32, (2 * RW_N, LANES), 0)
        col = lax.broadcasted_iota(jnp.int32, (2 * RW_N, LANES), 1)
        diag_blocks = (row < RW_N) == (col < RW_N)
        t_row = lax.broadcasted_iota(jnp.int32, (CHUNK, LANES), 0)
        t_col = lax.broadcasted_iota(jnp.int32, (CHUNK, LANES), 1) % CHUNK
        eye2 = (t_row == t_col).astype(F32)
        ka = ka_ref[...]
        bd = functools.partial(_pair_block_diag, even=even)

        units = []
        for d, refs in enumerate(((rf_ref, kf_ref, vf_ref, kkf_ref, lwf_ref, alf_ref),
                                  (rb_ref, kb_ref, vb_ref, kkb_ref, lwb_ref, alb_ref))):
            r_ref, k_ref, v_ref, kk_ref, lw_ref, al_ref = refs
            rev = jnp.logical_and(d == 1, n > META_CHUNK)
            incl_m, _ = _order_masks(rev)
            ahead2 = (t_col - t_row) * jnp.where(rev, -1, 1)
            lw = lw_ref[0, 0]
            al = al_ref[0, 0]
            kk = kk_ref[0]
            incl = _mm_exact_lhs(incl_m.astype(BF16), lw)
            tot = jnp.sum(lw, axis=0, keepdims=True)
            e_inv = jnp.exp(-incl)
            ptot = jnp.exp(tot)
            kmod = k_ref[0] * (1.0 + (al - 1.0) * ka)
            bt = (kk * al) * e_inv
            kt = kmod * e_inv
            common = dict(
                d=d, incl2=ahead2 <= 0, strict2=ahead2 < 0, ptot=ptot,
                at=(-kk * jnp.exp(incl - lw)).astype(BF16),
                rt=(r_ref[0] * jnp.exp(incl)).astype(BF16),
                bt=bt.astype(BF16), kt=kt.astype(BF16),
                bp=(bt * ptot).astype(BF16), kp=(kt * ptot).astype(BF16),
                v=v_ref[0].astype(BF16))
            for p in range(RW_PAIRS):
                units.append(dict(common, p=p, sl=slice(p * LANES, (p + 1) * LANES)))

        dot = functools.partial(jnp.dot, preferred_element_type=F32)
        for u in units:
            sl = u["sl"]
            lhs = jnp.concatenate([u["at"][:, sl], u["rt"][:, sl]], axis=0)
            rhs = jnp.concatenate([bd(u["bt"][:, sl]), bd(u["kt"][:, sl])], axis=0)
            g = lax.dot_general(lhs, rhs, _NT, preferred_element_type=F32)
            u["n"] = jnp.where(u["strict2"], g[:CHUNK, :LANES], 0.0)
            a_ak = jnp.where(u["strict2"], g[:CHUNK, LANES:], 0.0)
            u["a_r"] = jnp.where(jnp.concatenate([u["incl2"], u["incl2"]], axis=1), g[CHUNK:], 0.0).astype(BF16)
            u["a_ak"] = a_ak.astype(BF16)
        for u in units:
            u["av"] = dot(u["a_ak"], bd(u["v"][:, u["sl"]]))
        steps = int(math.log2(CHUNK))
        for u in units:
            u["t"] = eye2 + u["n"]
        for s in range(steps):
            for u in units:
                nb = u["n"].astype(BF16)
                if s == 0:
                    u["n"] = dot(nb, bd(nb))
                elif s < steps - 1:
                    zz = dot(nb, jnp.concatenate([bd(u["t"].astype(BF16)), bd(nb)], axis=1))
                    u["t"] = u["t"] + zz[:, :LANES]
                    u["n"] = zz[:, LANES:]
                else:
                    u["t"] = u["t"] + dot(nb, bd(u["t"].astype(BF16)))
        for u in units:
            rhs = jnp.concatenate([bd(u["at"][:, u["sl"]]), bd(u["av"].astype(BF16))], axis=1)
            u["wx"] = dot(u["t"].astype(BF16), rhs)
        for u in units:
            sl = u["sl"]
            u["h"] = ht_ref[u["d"], :, sl]
            lhs = jnp.concatenate([u["wx"][:, :LANES].astype(BF16), u["rt"][:, sl]], axis=0)
            ru = lax.dot_general(lhs, u["h"].astype(BF16), _NT, preferred_element_type=F32)
            u["u"] = (ru[:CHUNK] + u["wx"][:, LANES:]).astype(BF16)
            u["rh"] = ru[CHUNK:]
        for u in units:
            sl = u["sl"]
            rhs = jnp.concatenate([bd(u["u"]), bd(u["v"][:, sl])], axis=0)
            u["y"] = u["rh"] + dot(u["a_r"], rhs)
            dh = lax.dot_general(jnp.concatenate([u["u"], u["v"][:, sl]], axis=0),
                                 jnp.concatenate([u["bp"][:, sl], u["kp"][:, sl]], axis=0),
                                 _TN, preferred_element_type=F32)
            ht_ref[u["d"], :, sl] = u["h"] * u["ptot"][:, sl] + jnp.where(diag_blocks, dh, 0.0)
        for d, y_ref in enumerate((yf_ref, yb_ref)):
            y_ref[0] = jnp.concatenate([u["y"] for u in units if u["d"] == d], axis=-1)


def _rw_scan(r, k, v, kk, lw, al, k_a):
    bsz, lp, _ = r.shape
    nchunks = lp // CHUNK
    back = lambda n: _chunk_of(1, n, nchunks)
    tok_f = pl.BlockSpec((1, CHUNK, BRANCH_W), lambda b, n: (b, n, 0))
    tok_b = pl.BlockSpec((1, CHUNK, BRANCH_W), lambda b, n: (b, back(n), 0))
    dir_f = pl.BlockSpec((1, 1, CHUNK, BRANCH_W), lambda b, n: (0, b, n, 0))
    dir_b = pl.BlockSpec((1, 1, CHUNK, BRANCH_W), lambda b, n: (1, b, back(n), 0))
    sd = jax.ShapeDtypeStruct((bsz, lp, BRANCH_W), F32)
    yf, yb = pl.pallas_call(
        functools.partial(_rw_scan_kernel, nchunks=nchunks),
        grid=(bsz, nchunks),
        in_specs=[tok_f, tok_f, tok_f, tok_f, dir_f, dir_f,
                  tok_b, tok_b, tok_b, tok_b, dir_b, dir_b,
                  pl.BlockSpec((1, BRANCH_W), lambda b, n: (0, 0))],
        out_specs=[tok_f, tok_b],
        out_shape=[sd, sd],
        scratch_shapes=[pltpu.VMEM((2, 2 * RW_N, BRANCH_W), F32)],
        compiler_params=_cparams(("parallel", "arbitrary")),
        name="rw_scan",
    )(r, k, v, kk, lw, al, r, k, v, kk, lw, al, k_a)
    return yf, yb


def _out_kernel(h_ref, ona_ref, og_ref, gg_ref, yf_ref, yb_ref, bonus_ref, gate_ref, wout_ref, gng_ref,
                avg_ref, lng_ref, lnb_ref, fin_ref, o_ref, *, final, tile_off):
    i = pl.program_id(1) + tile_off
    og = og_ref[0, 0] + og_ref[1, 0]
    parts = []
    for h in range(GLA_HEADS):
        oh = og[:, h * GLA_DV:(h + 1) * GLA_DV]
        ms = jnp.mean(oh * oh, axis=-1, keepdims=True)
        parts.append(oh * lax.rsqrt(ms + GLA_NORM_EPS) * gng_ref[...])
    o_gla = jnp.concatenate(parts, axis=-1) * _silu(gg_ref[0])
    y = yf_ref[0] + yb_ref[0]
    mu = _mm_exact_rhs(y, avg_ref[...])
    yc = y - mu
    var = _mm_exact_rhs(yc * yc, avg_ref[...])
    yn = yc * lax.rsqrt(var + RW_GN_EPS) * lng_ref[...] + lnb_ref[...]
    o_rw = (yn + bonus_ref[0]) * _silu(gate_ref[0])
    cat = jnp.concatenate([ona_ref[0].astype(BF16), o_gla.astype(BF16), o_rw.astype(BF16)], axis=-1)
    hn = h_ref[0] + jnp.dot(cat, wout_ref[...], preferred_element_type=F32)
    if final:
        ms = jnp.mean(hn * hn, axis=-1, keepdims=True)
        o_ref[0] = hn * lax.rsqrt(ms + NORM_EPS) * fin_ref[...]
    else:
        rowid = lax.broadcasted_iota(jnp.int32, (ROW_TILE, 1), 0)
        live = jnp.logical_or(i > 0, rowid >= META_ROW0)
        o_ref[0] = jnp.where(live, hn, 0.0)


def _out_proj(hp, ona, ogla, zgla, yrw, bonus, zrw, wout, gla_norm_g, avg_bd, ln_g, ln_b, fin_g, final):
    bsz, lp, _ = hp.shape
    nb = lp // ROW_TILE
    off = 1 if final else 0
    nt = nb - off
    tile = lambda w, cb: pl.BlockSpec((1, ROW_TILE, w), lambda b, i: (b, i + off, cb))
    dtile = pl.BlockSpec((2, 1, ROW_TILE, BRANCH_W), lambda b, i: (0, b, i + off, 0))
    full = lambda a: pl.BlockSpec(a.shape, lambda b, i: (0,) * a.ndim)
    gla_gate_blk = (2 * GLA_KW + BRANCH_W) // BRANCH_W
    rw_gate_blk = 3
    out_rows = lp - off * ROW_TILE
    return pl.pallas_call(
        functools.partial(_out_kernel, final=final, tile_off=off),
        grid=(bsz, nt),
        in_specs=[
            tile(D_MODEL, 0), tile(BRANCH_W, 0), dtile, tile(BRANCH_W, gla_gate_blk),
            tile(BRANCH_W, 0), tile(BRANCH_W, 0), tile(BRANCH_W, 0), tile(BRANCH_W, rw_gate_blk),
            full(wout), full(gla_norm_g), full(avg_bd), full(ln_g), full(ln_b), full(fin_g),
        ],
        out_specs=pl.BlockSpec((1, ROW_TILE, D_MODEL), lambda b, i: (b, i, 0)),
        out_shape=jax.ShapeDtypeStruct((bsz, out_rows, D_MODEL), F32),
        compiler_params=_cparams(("parallel", "parallel")),
        name="out_proj",
    )(hp, ona, ogla, zgla, yrw[0], yrw[1], bonus, zrw, wout, gla_norm_g, avg_bd, ln_g, ln_b, fin_g)


def _block_diag(n, blk, val):
    idx = np.arange(n) // blk
    return jnp.asarray((idx[:, None] == idx[None, :]).astype(np.float32) * val)


def _pad_rows_per_dir(w, rows_total):
    _, r, c = w.shape
    out = jnp.zeros((2, rows_total, c), w.dtype)
    out = out.at[0, 0:r].set(w[0])
    out = out.at[1, r:2 * r].set(w[1])
    return out


def kernel(x, meta, norm_g, w_in, w_out, na_rpb, gla_g_up, gla_g_b, gla_norm_g, rw_conv, rw_w0, rw_w_up,
           rw_a0, rw_a_up, rw_k_k, rw_k_a, rw_r_k, rw_ln_g, rw_ln_b, rw_v0, rw_v_down, rw_v_up, final_norm_g):
    bsz, t, dm = x.shape
    assert dm == D_MODEL and t % ROW_TILE == 0 and t // GRID_W >= WIN_R
    depth = w_in.shape[0]
    lp = t + PAD
    rows = t // GRID_W
    hp = jnp.concatenate([
        jnp.zeros((bsz, META_ROW0, dm), x.dtype),
        jnp.broadcast_to(meta[None].astype(x.dtype), (bsz, N_META, dm)),
        x], axis=1)
    ones_bd = _block_diag(BRANCH_W, RW_N, 1.0).astype(BF16)
    avg_bd = _block_diag(BRANCH_W, RW_N, 1.0 / RW_N).astype(BF16)
    nsh = 3 * BRANCH_W
    vfirst = None
    out = None
    for l in range(depth):
        w = w_in[l]
        wna = w[:, :NA_COLS].astype(BF16)
        wg = w[:, NA_COLS:NA_COLS + GLA_COLS]
        wgla = jnp.pad(wg, ((0, 0), (0, GLA_ZCOLS - GLA_COLS))).astype(BF16)
        wr = w[:, NA_COLS + GLA_COLS:]
        wrw = jnp.concatenate([wr[:, :nsh], wr[:, RW_SHIFT_COLS:], wr[:, nsh:RW_SHIFT_COLS]], axis=1).astype(BF16)
        conv_w = rw_conv[l]
        zna, zgla, zrw = _in_proj(hp.reshape(bsz * lp, dm), norm_g[l][None], wna, wgla, wrw)
        zna = zna.reshape(bsz, lp, -1)
        zgla = zgla.reshape(bsz, lp, -1)
        zrw = zrw.reshape(bsz, lp, -1)

        ona = _na_branch(zna, _na_bias_table(na_rpb[l]), rows)

        gup_pad = _pad_rows_per_dir(gla_g_up[l], LANES)
        ogla = _gla_branch(zgla, gup_pad, gla_g_b[l][:, None, :])

        wup_pad = _pad_rows_per_dir(rw_w_up[l], 2 * RW_W_LORA)
        aup_pad = _pad_rows_per_dir(rw_a_up[l], 2 * RW_A_LORA)
        if l == 0:
            vmix = None
        else:
            vdn = jnp.pad(rw_v_down[l - 1], ((0, 0), (0, LANES - RW_V_LORA)))
            vup = jnp.pad(rw_v_up[l - 1], ((0, LANES - RW_V_LORA), (0, 0)))
            vmix = (rw_v0[l - 1][None], vdn, vup)
        r_, k_, v_, kk_, lw_, al_, bonus = _rw_prep(
            zrw, vfirst, conv_w, rw_w0[l][:, None, :], wup_pad, rw_a0[l][:, None, :], aup_pad,
            rw_k_k[l][None], rw_k_a[l][None], rw_r_k[l][None], ones_bd, vmix)
        if l == 0:
            vfirst = v_
        yrw = _rw_scan(r_, k_, v_, kk_, lw_, al_, rw_k_a[l][None])

        final = l == depth - 1
        res = _out_proj(hp, ona, ogla, zgla, yrw, bonus, zrw, w_out[l].astype(BF16), gla_norm_g[l][None],
                        avg_bd, rw_ln_g[l][None], rw_ln_b[l][None], final_norm_g[None], final)
        if final:
            out = res
        else:
            hp = res
    return out
```

```python
import functools
import math

import numpy as np
import jax
import jax.numpy as jnp
from jax import lax
from jax.experimental import pallas as pl
from jax.experimental.pallas import tpu as pltpu

F32 = jnp.float32
BF16 = jnp.bfloat16

D_MODEL = 1024
N_META = 16
GRID_W = 64
BRANCH_W = D_MODEL // 2
D_MIX = 3 * BRANCH_W
NORM_EPS = 1e-6
NA_HD = 64
NA_HEADS = BRANCH_W // NA_HD
WIN_R = 8
WIN_C = 16
GLA_HEADS = 4
GLA_DV = BRANCH_W // GLA_HEADS
GLA_DK = GLA_DV // 2
GLA_KW = GLA_HEADS * GLA_DK
GLA_LR = 16
GLA_NORMALIZER = 16.0
GLA_NORM_EPS = 1e-5
RW_N = 64
RW_HEADS = BRANCH_W // RW_N
RW_W_LORA = 64
RW_A_LORA = 64
RW_V_LORA = 32
RW_GN_EPS = 64e-5
NA_COLS = 4 * BRANCH_W
GLA_COLS = 2 * GLA_KW + 2 * BRANCH_W + 2 * GLA_LR
RW_SHIFT_COLS = 3 * BRANCH_W + 2 * RW_W_LORA + 2 * RW_A_LORA
RW_COLS = RW_SHIFT_COLS + BRANCH_W

LANES = 128
ROW_TILE = 512
CHUNK = 64
PAD = ROW_TILE
META_ROW0 = PAD - N_META
META_CHUNK = META_ROW0 // CHUNK
META_LOCAL0 = META_ROW0 - META_CHUNK * CHUNK
GLA_ZCOLS = 2 * GLA_KW + 2 * BRANCH_W + LANES
RW_LORA_COLS = 2 * RW_W_LORA + 2 * RW_A_LORA
NEG_BIG = -1e30
VMEM_LIMIT = 56 * 1024 * 1024

_NT = (((1,), (1,)), ((), ()))
_TN = (((0,), (0,)), ((), ()))


def _mm(a, b):
    return jnp.dot(a.astype(BF16), b.astype(BF16), preferred_element_type=F32)


def _mm_nt(a, b):
    return lax.dot_general(a.astype(BF16), b.astype(BF16), _NT, preferred_element_type=F32)


def _mm_tn(a, b):
    return lax.dot_general(a.astype(BF16), b.astype(BF16), _TN, preferred_element_type=F32)


def _split2(x):
    hi = x.astype(BF16)
    lo = (x - hi.astype(F32)).astype(BF16)
    return hi, lo


def _split3(x):
    hi = x.astype(BF16)
    r = x - hi.astype(F32)
    mid = r.astype(BF16)
    lo = (r - mid.astype(F32)).astype(BF16)
    return hi, mid, lo


def _mm3(a, b):
    ah, al = _split2(a)
    bh, bl = _split2(b)
    d = functools.partial(jnp.dot, preferred_element_type=F32)
    return d(ah, bh) + (d(ah, bl) + d(al, bh))


def _mm_exact_lhs(a_bf16, x):
    xh, xm, xl = _split3(x)
    d = functools.partial(jnp.dot, preferred_element_type=F32)
    return d(a_bf16, xh) + (d(a_bf16, xm) + d(a_bf16, xl))


def _mm_exact_rhs(x, b_bf16):
    xh, xm, xl = _split3(x)
    d = functools.partial(jnp.dot, preferred_element_type=F32)
    return d(xh, b_bf16) + (d(xm, b_bf16) + d(xl, b_bf16))


def _sigmoid(x):
    return 1.0 / (1.0 + jnp.exp(-x))


def _silu(x):
    return x * _sigmoid(x)


def _softplus(x):
    return jnp.maximum(x, 0.0) + jnp.log(1.0 + jnp.exp(-jnp.abs(x)))


def _cparams(sem):
    return pltpu.CompilerParams(dimension_semantics=sem, vmem_limit_bytes=VMEM_LIMIT)


def _inproj_kernel(h_ref, g_ref, wna_ref, wgla_ref, wrw_ref, zna_ref, zgla_ref, zrw_ref):
    h = h_ref[...]
    ms = jnp.mean(h * h, axis=-1, keepdims=True)
    hn = (h * lax.rsqrt(ms + NORM_EPS) * g_ref[...]).astype(BF16)
    zna_ref[...] = jnp.dot(hn, wna_ref[...], preferred_element_type=F32)
    zgla_ref[...] = jnp.dot(hn, wgla_ref[...], preferred_element_type=F32)
    zrw_ref[...] = jnp.dot(hn, wrw_ref[...], preferred_element_type=F32)


def _in_proj(h2d, norm_g, wna, wgla, wrw):
    n = h2d.shape[0]
    tm = ROW_TILE // 2
    const = lambda i: (0, 0)
    row = lambda i: (i, 0)
    return pl.pallas_call(
        _inproj_kernel,
        grid=(n // tm,),
        in_specs=[
            pl.BlockSpec((tm, D_MODEL), row),
            pl.BlockSpec((1, D_MODEL), const),
            pl.BlockSpec(wna.shape, const),
            pl.BlockSpec(wgla.shape, const),
            pl.BlockSpec(wrw.shape, const),
        ],
        out_specs=[
            pl.BlockSpec((tm, wna.shape[1]), row),
            pl.BlockSpec((tm, wgla.shape[1]), row),
            pl.BlockSpec((tm, wrw.shape[1]), row),
        ],
        out_shape=[
            jax.ShapeDtypeStruct((n, wna.shape[1]), F32),
            jax.ShapeDtypeStruct((n, wgla.shape[1]), F32),
            jax.ShapeDtypeStruct((n, wrw.shape[1]), F32),
        ],
        compiler_params=_cparams(("parallel",)),
        name="in_proj",
    )(h2d, norm_g, wna, wgla, wrw)


NA_QROWS = ROW_TILE // GRID_W


def _na_kernel(q_ref, g_ref, kp_ref, kc_ref, kn_ref, vp_ref, vc_ref, vn_ref, km_ref, vm_ref,
               bias_ref, o_ref, ks_ref, vs_ref, *, rows):
    i = pl.program_id(1)
    scale = NA_HD ** -0.5
    km = km_ref[0].astype(BF16)
    vm = vm_ref[0].astype(BF16)

    @pl.when(i == 0)
    def _meta_tile():
        o_ref[0, 0:META_ROW0, :] = jnp.zeros((META_ROW0, BRANCH_W), F32)
        qm = (q_ref[0, META_ROW0:PAD, :] * scale).astype(BF16)
        outs = []
        for h in range(NA_HEADS):
            sl = slice(h * NA_HD, (h + 1) * NA_HD)
            s = lax.dot_general(qm[:, sl], km[:, sl], _NT, preferred_element_type=F32)
            m = jnp.max(s, axis=-1, keepdims=True)
            p = jnp.exp(s - m)
            den = jnp.sum(p, axis=-1, keepdims=True)
            outs.append(jnp.dot(p.astype(BF16), vm[:, sl], preferred_element_type=F32) / den)
        om = jnp.concatenate(outs, axis=-1)
        o_ref[0, META_ROW0:PAD, :] = om * _silu(g_ref[0, META_ROW0:PAD, :])

    @pl.when(i > 0)
    def _real_tile():
        ks_ref[0:ROW_TILE, :] = kp_ref[0].astype(BF16)
        ks_ref[ROW_TILE:2 * ROW_TILE, :] = kc_ref[0].astype(BF16)
        ks_ref[2 * ROW_TILE:3 * ROW_TILE, :] = kn_ref[0].astype(BF16)
        vs_ref[0:ROW_TILE, :] = vp_ref[0].astype(BF16)
        vs_ref[ROW_TILE:2 * ROW_TILE, :] = vc_ref[0].astype(BF16)
        vs_ref[2 * ROW_TILE:3 * ROW_TILE, :] = vn_ref[0].astype(BF16)
        r0 = (i - 1) * NA_QROWS

        def row_body(j, carry):
            r = r0 + j
            rs = jnp.clip(r - WIN_R // 2, 0, rows - WIN_R)
            start = pl.multiple_of((rs - r0 + NA_QROWS) * GRID_W, GRID_W)
            pat = r - rs
            qoff = pl.multiple_of(j * GRID_W, GRID_W)
            qj = (q_ref[0, pl.ds(qoff, GRID_W), :] * scale).astype(BF16)
            kw = ks_ref[pl.ds(start, WIN_R * GRID_W), :]
            vw = vs_ref[pl.ds(start, WIN_R * GRID_W), :]
            heads = [slice(h * NA_HD, (h + 1) * NA_HD) for h in range(NA_HEADS)]
            scores = [(lax.dot_general(qj[:, sl], kw[:, sl], _NT, preferred_element_type=F32) + bias_ref[h, pat],
                       lax.dot_general(qj[:, sl], km[:, sl], _NT, preferred_element_type=F32))
                      for h, sl in enumerate(heads)]
            probs = []
            for s, sm in scores:
                m = jnp.maximum(jnp.max(s, axis=-1, keepdims=True),
                                jnp.max(sm, axis=-1, keepdims=True))
                p = jnp.exp(s - m)
                pm = jnp.exp(sm - m)
                den = jnp.sum(p, axis=-1, keepdims=True) + jnp.sum(pm, axis=-1, keepdims=True)
                probs.append((p.astype(BF16), pm.astype(BF16), den))
            outs = [(jnp.dot(p, vw[:, sl], preferred_element_type=F32)
                     + jnp.dot(pm, vm[:, sl], preferred_element_type=F32)) / den
                    for (p, pm, den), sl in zip(probs, heads)]
            oj = jnp.concatenate(outs, axis=-1)
            o_ref[0, pl.ds(qoff, GRID_W), :] = oj * _silu(g_ref[0, pl.ds(qoff, GRID_W), :])
            return carry

        lax.fori_loop(0, NA_QROWS, row_body, 0)


def _na_bias_table(rpb):
    cols = np.arange(GRID_W)
    cstart = np.clip(cols - WIN_C // 2, 0, GRID_W - WIN_C)
    kc = np.arange(GRID_W)
    inwin = (kc[None, :] >= cstart[:, None]) & (kc[None, :] < cstart[:, None] + WIN_C)
    dc = np.clip(kc[None, :] - cols[:, None] + (WIN_C - 1), 0, 2 * WIN_C - 2)
    pats = np.arange(WIN_R)
    jj = np.arange(WIN_R)
    dr = jj[None, :] - pats[:, None] + (WIN_R - 1)
    b = rpb[:, dr][:, :, :, dc]
    b = jnp.where(inwin[None, None, None], b, NEG_BIG)
    b = b.transpose(0, 1, 3, 2, 4).reshape(NA_HEADS, WIN_R, GRID_W, WIN_R * GRID_W)
    return b.astype(F32)


def _na_branch(zna, bias, rows):
    bsz, lp, _ = zna.shape
    nb = lp // ROW_TILE
    mblk = META_ROW0 // N_META
    tile = (1, ROW_TILE, BRANCH_W)
    prev = lambda b, i: jnp.maximum(i - 1, 0)
    nxt = lambda b, i: jnp.minimum(i + 1, nb - 1)
    in_specs = [
        pl.BlockSpec(tile, lambda b, i: (b, i, 0)),
        pl.BlockSpec(tile, lambda b, i: (b, i, 3)),
        pl.BlockSpec(tile, lambda b, i: (b, prev(b, i), 1)),
        pl.BlockSpec(tile, lambda b, i: (b, i, 1)),
        pl.BlockSpec(tile, lambda b, i: (b, nxt(b, i), 1)),
        pl.BlockSpec(tile, lambda b, i: (b, prev(b, i), 2)),
        pl.BlockSpec(tile, lambda b, i: (b, i, 2)),
        pl.BlockSpec(tile, lambda b, i: (b, nxt(b, i), 2)),
        pl.BlockSpec((1, N_META, BRANCH_W), lambda b, i: (b, mblk, 1)),
        pl.BlockSpec((1, N_META, BRANCH_W), lambda b, i: (b, mblk, 2)),
        pl.BlockSpec(bias.shape, lambda b, i: (0, 0, 0, 0)),
    ]
    return pl.pallas_call(
        functools.partial(_na_kernel, rows=rows),
        grid=(bsz, nb),
        in_specs=in_specs,
        out_specs=pl.BlockSpec(tile, lambda b, i: (b, i, 0)),
        out_shape=jax.ShapeDtypeStruct((bsz, lp, BRANCH_W), F32),
        scratch_shapes=[pltpu.VMEM((3 * ROW_TILE, BRANCH_W), BF16),
                        pltpu.VMEM((3 * ROW_TILE, BRANCH_W), BF16)],
        compiler_params=_cparams(("parallel", "parallel")),
        name="na_attn",
    )(zna, zna, zna, zna, zna, zna, zna, zna, zna, zna, bias)


def _chunk_of(d, n, nchunks):
    back = jnp.where(n <= META_CHUNK, n, nchunks + META_CHUNK - n)
    return jnp.where(d == 0, n, back)


def _order_masks(rev):
    row = lax.broadcasted_iota(jnp.int32, (CHUNK, CHUNK), 0)
    col = lax.broadcasted_iota(jnp.int32, (CHUNK, CHUNK), 1)
    ahead = (col - row) * jnp.where(rev, -1, 1)
    return ahead <= 0, ahead < 0


GLA_PAIRS = GLA_HEADS // 2


def _gla_kernel(zf_ref, zb_ref, gup_ref, gb_ref, of_ref, ob_ref, st_ref, *, nchunks):
    n = pl.program_id(1)

    @pl.when(n == 0)
    def _init():
        st_ref[...] = jnp.zeros_like(st_ref)

    @pl.when(n < META_CHUNK)
    def _padding():
        of_ref[0] = jnp.zeros((CHUNK, BRANCH_W), F32)
        ob_ref[0] = jnp.zeros((CHUNK, BRANCH_W), F32)

    @pl.when(n >= META_CHUNK)
    def _chunk():
        lane = lax.broadcasted_iota(jnp.int32, (1, LANES), 1)
        even = lane < GLA_DK
        row = lax.broadcasted_iota(jnp.int32, (2 * GLA_DV, LANES), 0)
        col = lax.broadcasted_iota(jnp.int32, (2 * GLA_DV, LANES), 1)
        diag_blocks = (row < GLA_DV) == (col < GLA_DK)
        t_row = lax.broadcasted_iota(jnp.int32, (CHUNK, LANES), 0)
        t_col = lax.broadcasted_iota(jnp.int32, (CHUNK, LANES), 1) % CHUNK
        rowid = lax.broadcasted_iota(jnp.int32, (CHUNK, 1), 0)
        live = jnp.logical_or(n > META_CHUNK, rowid >= META_LOCAL0)
        zero_v = jnp.zeros((CHUNK, GLA_DV), BF16)
        dot = functools.partial(jnp.dot, preferred_element_type=F32)

        units = []
        for d, z_ref in enumerate((zf_ref, zb_ref)):
            rev = jnp.logical_and(d == 1, n > META_CHUNK)
            incl, _ = _order_masks(rev)
            incl2 = (t_col - t_row) * jnp.where(rev, -1, 1) <= 0
            z = z_ref[0]
            q = z[:, 0:GLA_KW] * (GLA_DK ** -0.5)
            k = z[:, GLA_KW:2 * GLA_KW]
            v = z[:, 2 * GLA_KW:2 * GLA_KW + BRANCH_W].astype(BF16)
            gd = z[:, 2 * GLA_KW + 2 * BRANCH_W:]
            logits = _mm3(gd, gup_ref[d]) + gb_ref[d]
            gk = (jnp.minimum(logits, 0.0) - jnp.log(1.0 + jnp.exp(-jnp.abs(logits)))) / GLA_NORMALIZER
            gk = jnp.where(live, gk, 0.0)
            bcum = _mm_exact_lhs(incl.astype(BF16), gk)
            tot = jnp.sum(gk, axis=0, keepdims=True)
            e_inv = jnp.exp(-bcum)
            dec = jnp.exp(tot)
            qe = (q * jnp.exp(bcum)).astype(BF16)
            ke = k * e_inv
            kd = (ke * dec).astype(BF16)
            ke = ke.astype(BF16)
            for p in range(GLA_PAIRS):
                sl = slice(p * LANES, (p + 1) * LANES)
                ve = v[:, (2 * p) * GLA_DV:(2 * p + 1) * GLA_DV]
                vo = v[:, (2 * p + 1) * GLA_DV:(2 * p + 2) * GLA_DV]
                units.append(dict(d=d, p=p, sl=sl, incl2=incl2, dec=dec[:, sl], qe=qe[:, sl], ke=ke[:, sl],
                                  kd=kd[:, sl], v2=jnp.concatenate([ve, vo], axis=1),
                                  vbd=jnp.concatenate([jnp.concatenate([ve, zero_v], axis=1),
                                                       jnp.concatenate([zero_v, vo], axis=1)], axis=0)))
        for u in units:
            a = lax.dot_general(u["qe"], _pair_block_diag(u["ke"], even), _NT, preferred_element_type=F32)
            u["a"] = jnp.where(u["incl2"], a, 0.0).astype(BF16)
        for u in units:
            u["st"] = st_ref[u["d"], :, u["sl"]]
            u["o"] = dot(u["a"], u["vbd"]) + lax.dot_general(u["qe"], u["st"].astype(BF16), _NT,
                                                              preferred_element_type=F32)
        for u in units:
            dst = lax.dot_general(u["v2"], u["kd"], _TN, preferred_element_type=F32)
            st_ref[u["d"], :, u["sl"]] = u["st"] * u["dec"] + jnp.where(diag_blocks, dst, 0.0)
        for d, o_ref in enumerate((of_ref, ob_ref)):
            o_ref[0] = jnp.concatenate([u["o"] for u in units if u["d"] == d], axis=-1)


def _gla_branch(zgla, gup_pad, gb):
    bsz, lp, zc = zgla.shape
    nchunks = lp // CHUNK
    back = lambda n: _chunk_of(1, n, nchunks)
    full = lambda a: pl.BlockSpec(a.shape, lambda b, n: (0,) * a.ndim)
    out_f = pl.BlockSpec((1, CHUNK, BRANCH_W), lambda b, n: (b, n, 0))
    out_b = pl.BlockSpec((1, CHUNK, BRANCH_W), lambda b, n: (b, back(n), 0))
    sd = jax.ShapeDtypeStruct((bsz, lp, BRANCH_W), F32)
    return pl.pallas_call(
        functools.partial(_gla_kernel, nchunks=nchunks),
        grid=(bsz, nchunks),
        in_specs=[
            pl.BlockSpec((1, CHUNK, zc), lambda b, n: (b, n, 0)),
            pl.BlockSpec((1, CHUNK, zc), lambda b, n: (b, back(n), 0)),
            full(gup_pad), full(gb),
        ],
        out_specs=[out_f, out_b],
        out_shape=[sd, sd],
        scratch_shapes=[pltpu.VMEM((2, 2 * GLA_DV, GLA_KW), F32)],
        compiler_params=_cparams(("parallel", "arbitrary")),
        name="gla_scan",
    )(zgla, zgla, gup_pad, gb)


HALO = 8


def _rw_prep_kernel(*refs, mix):
    if mix:
        (z_ref, zp_ref, zn_ref, vf_ref, conv_ref, w0_ref, wup_ref, a0_ref, aup_ref, kk_ref, ka_ref,
         rk_ref, ones_ref, v0_ref, vdn_ref, vup_ref,
         r_o, k_o, v_o, kkn_o, lw_o, al_o, bonus_o, sc_ref) = refs
    else:
        (z_ref, zp_ref, zn_ref, conv_ref, w0_ref, wup_ref, a0_ref, aup_ref, kk_ref, ka_ref,
         rk_ref, ones_ref,
         r_o, k_o, v_o, kkn_o, lw_o, al_o, bonus_o, sc_ref) = refs
    i = pl.program_id(1)
    last = pl.num_programs(1) - 1
    nsh = 3 * BRANCH_W
    sc_ref[HALO:HALO + ROW_TILE, 0:nsh] = z_ref[0, :, 0:nsh]
    sc_ref[HALO:HALO + ROW_TILE, nsh:] = z_ref[0, :, nsh + BRANCH_W:]
    sc_ref[0:HALO, 0:nsh] = zp_ref[0, :, 0:nsh]
    sc_ref[0:HALO, nsh:] = zp_ref[0, :, nsh + BRANCH_W:]
    nxt_keep = jnp.where(i == last, 0.0, 1.0)
    sc_ref[HALO + ROW_TILE:, 0:nsh] = zn_ref[0, :, 0:nsh] * nxt_keep
    sc_ref[HALO + ROW_TILE:, nsh:] = zn_ref[0, :, nsh + BRANCH_W:] * nxt_keep
    zm = sc_ref[HALO - 1:HALO - 1 + ROW_TILE, :]
    z0 = sc_ref[HALO:HALO + ROW_TILE, :]
    zp = sc_ref[HALO + 1:HALO + 1 + ROW_TILE, :]
    cw = conv_ref[...]
    zs = zm * cw[0:1] + z0 * cw[1:2] + zp * cw[2:3]
    r = zs[:, 0:BRANCH_W]
    k = zs[:, BRANCH_W:2 * BRANCH_W]
    v = zs[:, 2 * BRANCH_W:nsh]
    wd = zs[:, nsh:nsh + 2 * RW_W_LORA]
    ad = zs[:, nsh + 2 * RW_W_LORA:]
    rowid = lax.broadcasted_iota(jnp.int32, (ROW_TILE, 1), 0)
    live = jnp.logical_or(i > 0, rowid >= META_ROW0)
    twd = jnp.tanh(wd)
    als = []
    for d in range(2):
        wl = _mm3(twd, wup_ref[d]) + w0_ref[d]
        lw = -jnp.exp(-_softplus(-wl) - 0.5)
        lw_o[d, 0] = jnp.where(live, lw, 0.0)
        al = _sigmoid(_mm3(ad, aup_ref[d]) + a0_ref[d])
        al_o[d, 0] = al
        als.append(al)
    kkr = k * kk_ref[...]
    nrm2 = _mm_exact_rhs(kkr * kkr, ones_ref[...])
    kkn = kkr / jnp.maximum(jnp.sqrt(nrm2), 1e-12)
    ka = ka_ref[...]
    kmod0 = k * (1.0 + (als[0] - 1.0) * ka)
    kmod1 = k * (1.0 + (als[1] - 1.0) * ka)
    if mix:
        gate = _sigmoid(v0_ref[...] + _mm3(_mm3(v, vdn_ref[...]), vup_ref[...]))
        v = v + (vf_ref[0] - v) * gate
    kb = 0.5 * (kmod0 + kmod1)
    bsum = _mm_exact_rhs(r * kb * rk_ref[...], ones_ref[...])
    bonus_o[0] = bsum * v
    r_o[0] = r
    k_o[0] = jnp.where(live, k, 0.0)
    v_o[0] = jnp.where(live, v, 0.0)
    kkn_o[0] = jnp.where(live, kkn, 0.0)


def _rw_prep(zrw, vfirst, conv_w, w0, wup_pad, a0, aup_pad, k_k, k_a, r_k, ones_bd, vmix):
    bsz, lp, zc = zrw.shape
    nb = lp // ROW_TILE
    nhalo = lp // HALO
    per_tile = ROW_TILE // HALO
    mix = vmix is not None
    tile = pl.BlockSpec((1, ROW_TILE, BRANCH_W), lambda b, i: (b, i, 0))
    dtile = pl.BlockSpec((2, 1, ROW_TILE, BRANCH_W), lambda b, i: (0, b, i, 0))
    full = lambda a: pl.BlockSpec(a.shape, lambda b, i: (0,) * a.ndim)
    in_specs = [
        pl.BlockSpec((1, ROW_TILE, zc), lambda b, i: (b, i, 0)),
        pl.BlockSpec((1, HALO, zc), lambda b, i: (b, jnp.maximum(i * per_tile - 1, 0), 0)),
        pl.BlockSpec((1, HALO, zc), lambda b, i: (b, jnp.minimum((i + 1) * per_tile, nhalo - 1), 0)),
    ]
    args = [zrw, zrw, zrw]
    if mix:
        in_specs.append(tile)
        args.append(vfirst)
    params = [conv_w, w0, wup_pad, a0, aup_pad, k_k, k_a, r_k, ones_bd]
    if mix:
        params += list(vmix)
    in_specs += [full(p) for p in params]
    args += params
    sd = jax.ShapeDtypeStruct((bsz, lp, BRANCH_W), F32)
    sd2 = jax.ShapeDtypeStruct((2, bsz, lp, BRANCH_W), F32)
    return pl.pallas_call(
        functools.partial(_rw_prep_kernel, mix=mix),
        grid=(bsz, nb),
        in_specs=in_specs,
        out_specs=[tile, tile, tile, tile, dtile, dtile, tile],
        out_shape=[sd, sd, sd, sd, sd2, sd2, sd],
        scratch_shapes=[pltpu.VMEM((ROW_TILE + 2 * HALO, 3 * BRANCH_W + RW_LORA_COLS), F32)],
        compiler_params=_cparams(("parallel", "parallel")),
        name="rw_prep",
    )(*args)


RW_PAIRS = RW_HEADS // 2


def _pair_block_diag(x, even):
    z = jnp.zeros_like(x)
    return jnp.concatenate([jnp.where(even, x, z), jnp.where(even, z, x)], axis=0)


def _rw_scan_kernel(rf_ref, kf_ref, vf_ref, kkf_ref, lwf_ref, alf_ref,
                    rb_ref, kb_ref, vb_ref, kkb_ref, lwb_ref, alb_ref, ka_ref,
                    yf_ref, yb_ref, ht_ref, *, nchunks):
    n = pl.program_id(1)

    @pl.when(n == 0)
    def _init():
        ht_ref[...] = jnp.zeros_like(ht_ref)

    @pl.when(n < META_CHUNK)
    def _padding():
        yf_ref[0] = jnp.zeros((CHUNK, BRANCH_W), F32)
        yb_ref[0] = jnp.zeros((CHUNK, BRANCH_W), F32)

    @pl.when(n >= META_CHUNK)
    def _chunk():
        lane = lax.broadcasted_iota(jnp.int32, (1, LANES), 1)
        even = lane < RW_N
        row = lax.broadcasted_iota(jnp.int32, (2 * RW_N, LANES), 0)
        col = lax.broadcasted_iota(jnp.int32, (2 * RW_N, LANES), 1)
        diag_blocks = (row < RW_N) == (col < RW_N)
        t_row = lax.broadcasted_iota(jnp.int32, (CHUNK, LANES), 0)
        t_col = lax.broadcasted_iota(jnp.int32, (CHUNK, LANES), 1) % CHUNK
        eye2 = (t_row == t_col).astype(F32)
        ka = ka_ref[...]
        bd = functools.partial(_pair_block_diag, even=even)

        units = []
        for d, refs in enumerate(((rf_ref, kf_ref, vf_ref, kkf_ref, lwf_ref, alf_ref),
                                  (rb_ref, kb_ref, vb_ref, kkb_ref, lwb_ref, alb_ref))):
            r_ref, k_ref, v_ref, kk_ref, lw_ref, al_ref = refs
            rev = jnp.logical_and(d == 1, n > META_CHUNK)
            incl_m, _ = _order_masks(rev)
            ahead2 = (t_col - t_row) * jnp.where(rev, -1, 1)
            lw = lw_ref[0, 0]
            al = al_ref[0, 0]
            kk = kk_ref[0]
            incl = _mm_exact_lhs(incl_m.astype(BF16), lw)
            tot = jnp.sum(lw, axis=0, keepdims=True)
            e_inv = jnp.exp(-incl)
            ptot = jnp.exp(tot)
            kmod = k_ref[0] * (1.0 + (al - 1.0) * ka)
            bt = (kk * al) * e_inv
            kt = kmod * e_inv
            common = dict(
                d=d, incl2=ahead2 <= 0, strict2=ahead2 < 0, ptot=ptot,
                at=(-kk * jnp.exp(incl - lw)).astype(BF16),
                rt=(r_ref[0] * jnp.exp(incl)).astype(BF16),
                bt=bt.astype(BF16), kt=kt.astype(BF16),
                bp=(bt * ptot).astype(BF16), kp=(kt * ptot).astype(BF16),
                v=v_ref[0].astype(BF16))
            for p in range(RW_PAIRS):
                units.append(dict(common, p=p, sl=slice(p * LANES, (p + 1) * LANES)))

        dot = functools.partial(jnp.dot, preferred_element_type=F32)
        for u in units:
            sl = u["sl"]
            lhs = jnp.concatenate([u["at"][:, sl], u["rt"][:, sl]], axis=0)
            rhs = jnp.concatenate([bd(u["bt"][:, sl]), bd(u["kt"][:, sl])], axis=0)
            g = lax.dot_general(lhs, rhs, _NT, preferred_element_type=F32)
            u["n"] = jnp.where(u["strict2"], g[:CHUNK, :LANES], 0.0)
            a_ak = jnp.where(u["strict2"], g[:CHUNK, LANES:], 0.0)
            u["a_r"] = jnp.where(jnp.concatenate([u["incl2"], u["incl2"]], axis=1), g[CHUNK:], 0.0).astype(BF16)
            u["a_ak"] = a_ak.astype(BF16)
        for u in units:
            u["av"] = dot(u["a_ak"], bd(u["v"][:, u["sl"]]))
        steps = int(math.log2(CHUNK))
        for u in units:
            u["t"] = eye2 + u["n"]
        for s in range(steps):
            for u in units:
                nb = u["n"].astype(BF16)
                if s == 0:
                    u["n"] = dot(nb, bd(nb))
                elif s < steps - 1:
                    zz = dot(nb, jnp.concatenate([bd(u["t"].astype(BF16)), bd(nb)], axis=1))
                    u["t"] = u["t"] + zz[:, :LANES]
                    u["n"] = zz[:, LANES:]
                else:
                    u["t"] = u["t"] + dot(nb, bd(u["t"].astype(BF16)))
        for u in units:
            rhs = jnp.concatenate([bd(u["at"][:, u["sl"]]), bd(u["av"].astype(BF16))], axis=1)
            u["wx"] = dot(u["t"].astype(BF16), rhs)
        for u in units:
            sl = u["sl"]
            u["h"] = ht_ref[u["d"], :, sl]
            lhs = jnp.concatenate([u["wx"][:, :LANES].astype(BF16), u["rt"][:, sl]], axis=0)
            ru = lax.dot_general(lhs, u["h"].astype(BF16), _NT, preferred_element_type=F32)
            u["u"] = (ru[:CHUNK] + u["wx"][:, LANES:]).astype(BF16)
            u["rh"] = ru[CHUNK:]
        for u in units:
            sl = u["sl"]
            rhs = jnp.concatenate([bd(u["u"]), bd(u["v"][:, sl])], axis=0)
            u["y"] = u["rh"] + dot(u["a_r"], rhs)
            dh = lax.dot_general(jnp.concatenate([u["u"], u["v"][:, sl]], axis=0),
                                 jnp.concatenate([u["bp"][:, sl], u["kp"][:, sl]], axis=0),
                                 _TN, preferred_element_type=F32)
            ht_ref[u["d"], :, sl] = u["h"] * u["ptot"][:, sl] + jnp.where(diag_blocks, dh, 0.0)
        for d, y_ref in enumerate((yf_ref, yb_ref)):
            y_ref[0] = jnp.concatenate([u["y"] for u in units if u["d"] == d], axis=-1)


def _rw_scan(r, k, v, kk, lw, al, k_a):
    bsz, lp, _ = r.shape
    nchunks = lp // CHUNK
    back = lambda n: _chunk_of(1, n, nchunks)
    tok_f = pl.BlockSpec((1, CHUNK, BRANCH_W), lambda b, n: (b, n, 0))
    tok_b = pl.BlockSpec((1, CHUNK, BRANCH_W), lambda b, n: (b, back(n), 0))
    dir_f = pl.BlockSpec((1, 1, CHUNK, BRANCH_W), lambda b, n: (0, b, n, 0))
    dir_b = pl.BlockSpec((1, 1, CHUNK, BRANCH_W), lambda b, n: (1, b, back(n), 0))
    sd = jax.ShapeDtypeStruct((bsz, lp, BRANCH_W), F32)
    yf, yb = pl.pallas_call(
        functools.partial(_rw_scan_kernel, nchunks=nchunks),
        grid=(bsz, nchunks),
        in_specs=[tok_f, tok_f, tok_f, tok_f, dir_f, dir_f,
                  tok_b, tok_b, tok_b, tok_b, dir_b, dir_b,
                  pl.BlockSpec((1, BRANCH_W), lambda b, n: (0, 0))],
        out_specs=[tok_f, tok_b],
        out_shape=[sd, sd],
        scratch_shapes=[pltpu.VMEM((2, 2 * RW_N, BRANCH_W), F32)],
        compiler_params=_cparams(("parallel", "arbitrary")),
        name="rw_scan",
    )(r, k, v, kk, lw, al, r, k, v, kk, lw, al, k_a)
    return yf, yb


def _out_kernel(h_ref, ona_ref, ogf_ref, ogb_ref, gg_ref, yf_ref, yb_ref, bonus_ref, gate_ref, wout_ref, gng_ref,
                avg_ref, lng_ref, lnb_ref, fin_ref, o_ref, *, final, tile_off):
    i = pl.program_id(1) + tile_off
    og = ogf_ref[0] + ogb_ref[0]
    parts = []
    for h in range(GLA_HEADS):
        oh = og[:, h * GLA_DV:(h + 1) * GLA_DV]
        ms = jnp.mean(oh * oh, axis=-1, keepdims=True)
        parts.append(oh * lax.rsqrt(ms + GLA_NORM_EPS) * gng_ref[...])
    o_gla = jnp.concatenate(parts, axis=-1) * _silu(gg_ref[0])
    y = yf_ref[0] + yb_ref[0]
    mu = _mm_exact_rhs(y, avg_ref[...])
    yc = y - mu
    var = _mm_exact_rhs(yc * yc, avg_ref[...])
    yn = yc * lax.rsqrt(var + RW_GN_EPS) * lng_ref[...] + lnb_ref[...]
    o_rw = (yn + bonus_ref[0]) * _silu(gate_ref[0])
    cat = jnp.concatenate([ona_ref[0].astype(BF16), o_gla.astype(BF16), o_rw.astype(BF16)], axis=-1)
    hn = h_ref[0] + jnp.dot(cat, wout_ref[...], preferred_element_type=F32)
    if final:
        ms = jnp.mean(hn * hn, axis=-1, keepdims=True)
        o_ref[0] = hn * lax.rsqrt(ms + NORM_EPS) * fin_ref[...]
    else:
        rowid = lax.broadcasted_iota(jnp.int32, (ROW_TILE, 1), 0)
        live = jnp.logical_or(i > 0, rowid >= META_ROW0)
        o_ref[0] = jnp.where(live, hn, 0.0)


def _out_proj(hp, ona, ogla, zgla, yrw, bonus, zrw, wout, gla_norm_g, avg_bd, ln_g, ln_b, fin_g, final):
    bsz, lp, _ = hp.shape
    nb = lp // ROW_TILE
    off = 1 if final else 0
    nt = nb - off
    tile = lambda w, cb: pl.BlockSpec((1, ROW_TILE, w), lambda b, i: (b, i + off, cb))
    full = lambda a: pl.BlockSpec(a.shape, lambda b, i: (0,) * a.ndim)
    gla_gate_blk = (2 * GLA_KW + BRANCH_W) // BRANCH_W
    rw_gate_blk = 3
    out_rows = lp - off * ROW_TILE
    return pl.pallas_call(
        functools.partial(_out_kernel, final=final, tile_off=off),
        grid=(bsz, nt),
        in_specs=[
            tile(D_MODEL, 0), tile(BRANCH_W, 0), tile(BRANCH_W, 0), tile(BRANCH_W, 0),
            tile(BRANCH_W, gla_gate_blk),
            tile(BRANCH_W, 0), tile(BRANCH_W, 0), tile(BRANCH_W, 0), tile(BRANCH_W, rw_gate_blk),
            full(wout), full(gla_norm_g), full(avg_bd), full(ln_g), full(ln_b), full(fin_g),
        ],
        out_specs=pl.BlockSpec((1, ROW_TILE, D_MODEL), lambda b, i: (b, i, 0)),
        out_shape=jax.ShapeDtypeStruct((bsz, out_rows, D_MODEL), F32),
        compiler_params=_cparams(("parallel", "parallel")),
        name="out_proj",
    )(hp, ona, ogla[0], ogla[1], zgla, yrw[0], yrw[1], bonus, zrw, wout, gla_norm_g, avg_bd, ln_g, ln_b, fin_g)


def _block_diag(n, blk, val):
    idx = np.arange(n) // blk
    return jnp.asarray((idx[:, None] == idx[None, :]).astype(np.float32) * val)


def _pad_rows_per_dir(w, rows_total):
    _, r, c = w.shape
    out = jnp.zeros((2, rows_total, c), w.dtype)
    out = out.at[0, 0:r].set(w[0])
    out = out.at[1, r:2 * r].set(w[1])
    return out


def kernel(x, meta, norm_g, w_in, w_out, na_rpb, gla_g_up, gla_g_b, gla_norm_g, rw_conv, rw_w0, rw_w_up,
           rw_a0, rw_a_up, rw_k_k, rw_k_a, rw_r_k, rw_ln_g, rw_ln_b, rw_v0, rw_v_down, rw_v_up, final_norm_g):
    bsz, t, dm = x.shape
    assert dm == D_MODEL and t % ROW_TILE == 0 and t // GRID_W >= WIN_R
    depth = w_in.shape[0]
    lp = t + PAD
    rows = t // GRID_W
    hp = jnp.concatenate([
        jnp.zeros((bsz, META_ROW0, dm), x.dtype),
        jnp.broadcast_to(meta[None].astype(x.dtype), (bsz, N_META, dm)),
        x], axis=1)
    ones_bd = _block_diag(BRANCH_W, RW_N, 1.0).astype(BF16)
    avg_bd = _block_diag(BRANCH_W, RW_N, 1.0 / RW_N).astype(BF16)
    nsh = 3 * BRANCH_W
    vfirst = None
    out = None
    for l in range(depth):
        w = w_in[l]
        wna = w[:, :NA_COLS].astype(BF16)
        wg = w[:, NA_COLS:NA_COLS + GLA_COLS]
        wgla = jnp.pad(wg, ((0, 0), (0, GLA_ZCOLS - GLA_COLS))).astype(BF16)
        wr = w[:, NA_COLS + GLA_COLS:]
        wrw = jnp.concatenate([wr[:, :nsh], wr[:, RW_SHIFT_COLS:], wr[:, nsh:RW_SHIFT_COLS]], axis=1).astype(BF16)
        conv_w = rw_conv[l]
        zna, zgla, zrw = _in_proj(hp.reshape(bsz * lp, dm), norm_g[l][None], wna, wgla, wrw)
        zna = zna.reshape(bsz, lp, -1)
        zgla = zgla.reshape(bsz, lp, -1)
        zrw = zrw.reshape(bsz, lp, -1)

        ona = _na_branch(zna, _na_bias_table(na_rpb[l]), rows)

        gup_pad = _pad_rows_per_dir(gla_g_up[l], LANES)
        ogla = _gla_branch(zgla, gup_pad, gla_g_b[l][:, None, :])

        wup_pad = _pad_rows_per_dir(rw_w_up[l], 2 * RW_W_LORA)
        aup_pad = _pad_rows_per_dir(rw_a_up[l], 2 * RW_A_LORA)
        if l == 0:
            vmix = None
        else:
            vdn = jnp.pad(rw_v_down[l - 1], ((0, 0), (0, LANES - RW_V_LORA)))
            vup = jnp.pad(rw_v_up[l - 1], ((0, LANES - RW_V_LORA), (0, 0)))
            vmix = (rw_v0[l - 1][None], vdn, vup)
        r_, k_, v_, kk_, lw_, al_, bonus = _rw_prep(
            zrw, vfirst, conv_w, rw_w0[l][:, None, :], wup_pad, rw_a0[l][:, None, :], aup_pad,
            rw_k_k[l][None], rw_k_a[l][None], rw_r_k[l][None], ones_bd, vmix)
        if l == 0:
            vfirst = v_
        yrw = _rw_scan(r_, k_, v_, kk_, lw_, al_, rw_k_a[l][None])

        final = l == depth - 1
        res = _out_proj(hp, ona, ogla, zgla, yrw, bonus, zrw, w_out[l].astype(BF16), gla_norm_g[l][None],
                        avg_bd, rw_ln_g[l][None], rw_ln_b[l][None], final_norm_g[None], final)
        if final:
            out = res
        else:
            hp = res
    return out
```

```python
import functools
import math

import numpy as np
import jax
import jax.numpy as jnp
from jax import lax
from jax.experimental import pallas as pl
from jax.experimental.pallas import tpu as pltpu

F32 = jnp.float32
BF16 = jnp.bfloat16

D_MODEL = 1024
N_META = 16
GRID_W = 64
BRANCH_W = D_MODEL // 2
D_MIX = 3 * BRANCH_W
NORM_EPS = 1e-6
NA_HD = 64
NA_HEADS = BRANCH_W // NA_HD
WIN_R = 8
WIN_C = 16
GLA_HEADS = 4
GLA_DV = BRANCH_W // GLA_HEADS
GLA_DK = GLA_DV // 2
GLA_KW = GLA_HEADS * GLA_DK
GLA_LR = 16
GLA_NORMALIZER = 16.0
GLA_NORM_EPS = 1e-5
RW_N = 64
RW_HEADS = BRANCH_W // RW_N
RW_W_LORA = 64
RW_A_LORA = 64
RW_V_LORA = 32
RW_GN_EPS = 64e-5
NA_COLS = 4 * BRANCH_W
GLA_COLS = 2 * GLA_KW + 2 * BRANCH_W + 2 * GLA_LR
RW_SHIFT_COLS = 3 * BRANCH_W + 2 * RW_W_LORA + 2 * RW_A_LORA
RW_COLS = RW_SHIFT_COLS + BRANCH_W

LANES = 128
ROW_TILE = 512
CHUNK = 64
PAD = ROW_TILE
META_ROW0 = PAD - N_META
META_CHUNK = META_ROW0 // CHUNK
META_LOCAL0 = META_ROW0 - META_CHUNK * CHUNK
GLA_ZCOLS = 2 * GLA_KW + 2 * BRANCH_W + LANES
RW_LORA_COLS = 2 * RW_W_LORA + 2 * RW_A_LORA
NEG_BIG = -1e30
VMEM_LIMIT = 56 * 1024 * 1024

_NT = (((1,), (1,)), ((), ()))
_TN = (((0,), (0,)), ((), ()))


def _mm(a, b):
    return jnp.dot(a.astype(BF16), b.astype(BF16), preferred_element_type=F32)


def _mm_nt(a, b):
    return lax.dot_general(a.astype(BF16), b.astype(BF16), _NT, preferred_element_type=F32)


def _mm_tn(a, b):
    return lax.dot_general(a.astype(BF16), b.astype(BF16), _TN, preferred_element_type=F32)


def _split2(x):
    hi = x.astype(BF16)
    lo = (x - hi.astype(F32)).astype(BF16)
    return hi, lo


def _split3(x):
    hi = x.astype(BF16)
    r = x - hi.astype(F32)
    mid = r.astype(BF16)
    lo = (r - mid.astype(F32)).astype(BF16)
    return hi, mid, lo


def _mm3(a, b):
    ah, al = _split2(a)
    bh, bl = _split2(b)
    d = functools.partial(jnp.dot, preferred_element_type=F32)
    return d(ah, bh) + (d(ah, bl) + d(al, bh))


def _mm_exact_lhs(a_bf16, x):
    xh, xl = _split2(x)
    d = functools.partial(jnp.dot, preferred_element_type=F32)
    return d(a_bf16, xh) + d(a_bf16, xl)


def _mm_exact_rhs(x, b_bf16):
    xh, xl = _split2(x)
    d = functools.partial(jnp.dot, preferred_element_type=F32)
    return d(xh, b_bf16) + d(xl, b_bf16)


def _sigmoid(x):
    return 1.0 / (1.0 + jnp.exp(-x))


def _silu(x):
    return x * _sigmoid(x)


def _softplus(x):
    return jnp.maximum(x, 0.0) + jnp.log(1.0 + jnp.exp(-jnp.abs(x)))


def _cparams(sem):
    return pltpu.CompilerParams(dimension_semantics=sem, vmem_limit_bytes=VMEM_LIMIT)


def _inproj_kernel(h_ref, g_ref, wna_ref, wgla_ref, wrw_ref, zna_ref, zgla_ref, zrw_ref):
    h = h_ref[...]
    ms = jnp.mean(h * h, axis=-1, keepdims=True)
    hn = (h * lax.rsqrt(ms + NORM_EPS) * g_ref[...]).astype(BF16)
    zna_ref[...] = jnp.dot(hn, wna_ref[...], preferred_element_type=F32)
    zgla_ref[...] = jnp.dot(hn, wgla_ref[...], preferred_element_type=F32)
    zrw_ref[...] = jnp.dot(hn, wrw_ref[...], preferred_element_type=F32)


def _in_proj(h2d, norm_g, wna, wgla, wrw):
    n = h2d.shape[0]
    tm = ROW_TILE // 2
    const = lambda i: (0, 0)
    row = lambda i: (i, 0)
    return pl.pallas_call(
        _inproj_kernel,
        grid=(n // tm,),
        in_specs=[
            pl.BlockSpec((tm, D_MODEL), row),
            pl.BlockSpec((1, D_MODEL), const),
            pl.BlockSpec(wna.shape, const),
            pl.BlockSpec(wgla.shape, const),
            pl.BlockSpec(wrw.shape, const),
        ],
        out_specs=[
            pl.BlockSpec((tm, wna.shape[1]), row),
            pl.BlockSpec((tm, wgla.shape[1]), row),
            pl.BlockSpec((tm, wrw.shape[1]), row),
        ],
        out_shape=[
            jax.ShapeDtypeStruct((n, wna.shape[1]), F32),
            jax.ShapeDtypeStruct((n, wgla.shape[1]), F32),
            jax.ShapeDtypeStruct((n, wrw.shape[1]), F32),
        ],
        compiler_params=_cparams(("parallel",)),
        name="in_proj",
    )(h2d, norm_g, wna, wgla, wrw)


NA_QROWS = ROW_TILE // GRID_W


def _na_kernel(q_ref, g_ref, kp_ref, kc_ref, kn_ref, vp_ref, vc_ref, vn_ref, km_ref, vm_ref,
               bias_ref, o_ref, ks_ref, vs_ref, *, rows):
    i = pl.program_id(1)
    scale = NA_HD ** -0.5
    km = km_ref[0].astype(BF16)
    vm = vm_ref[0].astype(BF16)

    @pl.when(i == 0)
    def _meta_tile():
        o_ref[0, 0:META_ROW0, :] = jnp.zeros((META_ROW0, BRANCH_W), F32)
        qm = (q_ref[0, META_ROW0:PAD, :] * scale).astype(BF16)
        outs = []
        for h in range(NA_HEADS):
            sl = slice(h * NA_HD, (h + 1) * NA_HD)
            s = lax.dot_general(qm[:, sl], km[:, sl], _NT, preferred_element_type=F32)
            m = jnp.max(s, axis=-1, keepdims=True)
            p = jnp.exp(s - m)
            den = jnp.sum(p, axis=-1, keepdims=True)
            outs.append(jnp.dot(p.astype(BF16), vm[:, sl], preferred_element_type=F32) / den)
        om = jnp.concatenate(outs, axis=-1)
        o_ref[0, META_ROW0:PAD, :] = om * _silu(g_ref[0, META_ROW0:PAD, :])

    @pl.when(i > 0)
    def _real_tile():
        ks_ref[0:ROW_TILE, :] = kp_ref[0].astype(BF16)
        ks_ref[ROW_TILE:2 * ROW_TILE, :] = kc_ref[0].astype(BF16)
        ks_ref[2 * ROW_TILE:3 * ROW_TILE, :] = kn_ref[0].astype(BF16)
        vs_ref[0:ROW_TILE, :] = vp_ref[0].astype(BF16)
        vs_ref[ROW_TILE:2 * ROW_TILE, :] = vc_ref[0].astype(BF16)
        vs_ref[2 * ROW_TILE:3 * ROW_TILE, :] = vn_ref[0].astype(BF16)
        r0 = (i - 1) * NA_QROWS

        def row_body(j, carry):
            r = r0 + j
            rs = jnp.clip(r - WIN_R // 2, 0, rows - WIN_R)
            start = pl.multiple_of((rs - r0 + NA_QROWS) * GRID_W, GRID_W)
            pat = r - rs
            qoff = pl.multiple_of(j * GRID_W, GRID_W)
            qj = (q_ref[0, pl.ds(qoff, GRID_W), :] * scale).astype(BF16)
            kw = ks_ref[pl.ds(start, WIN_R * GRID_W), :]
            vw = vs_ref[pl.ds(start, WIN_R * GRID_W), :]
            heads = [slice(h * NA_HD, (h + 1) * NA_HD) for h in range(NA_HEADS)]
            scores = [(lax.dot_general(qj[:, sl], kw[:, sl], _NT, preferred_element_type=F32) + bias_ref[h, pat],
                       lax.dot_general(qj[:, sl], km[:, sl], _NT, preferred_element_type=F32))
                      for h, sl in enumerate(heads)]
            probs = []
            for s, sm in scores:
                m = jnp.maximum(jnp.max(s, axis=-1, keepdims=True),
                                jnp.max(sm, axis=-1, keepdims=True))
                p = jnp.exp(s - m)
                pm = jnp.exp(sm - m)
                den = jnp.sum(p, axis=-1, keepdims=True) + jnp.sum(pm, axis=-1, keepdims=True)
                probs.append((p.astype(BF16), pm.astype(BF16), den))
            outs = [(jnp.dot(p, vw[:, sl], preferred_element_type=F32)
                     + jnp.dot(pm, vm[:, sl], preferred_element_type=F32)) / den
                    for (p, pm, den), sl in zip(probs, heads)]
            oj = jnp.concatenate(outs, axis=-1)
            o_ref[0, pl.ds(qoff, GRID_W), :] = oj * _silu(g_ref[0, pl.ds(qoff, GRID_W), :])
            return carry

        lax.fori_loop(0, NA_QROWS, row_body, 0, unroll=2)


def _na_bias_table(rpb):
    cols = np.arange(GRID_W)
    cstart = np.clip(cols - WIN_C // 2, 0, GRID_W - WIN_C)
    kc = np.arange(GRID_W)
    inwin = (kc[None, :] >= cstart[:, None]) & (kc[None, :] < cstart[:, None] + WIN_C)
    dc = np.clip(kc[None, :] - cols[:, None] + (WIN_C - 1), 0, 2 * WIN_C - 2)
    pats = np.arange(WIN_R)
    jj = np.arange(WIN_R)
    dr = jj[None, :] - pats[:, None] + (WIN_R - 1)
    b = rpb[:, dr][:, :, :, dc]
    b = jnp.where(inwin[None, None, None], b, NEG_BIG)
    b = b.transpose(0, 1, 3, 2, 4).reshape(NA_HEADS, WIN_R, GRID_W, WIN_R * GRID_W)
    return b.astype(F32)


def _na_branch(zna, bias, rows):
    bsz, lp, _ = zna.shape
    nb = lp // ROW_TILE
    mblk = META_ROW0 // N_META
    tile = (1, ROW_TILE, BRANCH_W)
    prev = lambda b, i: jnp.maximum(i - 1, 0)
    nxt = lambda b, i: jnp.minimum(i + 1, nb - 1)
    in_specs = [
        pl.BlockSpec(tile, lambda b, i: (b, i, 0)),
        pl.BlockSpec(tile, lambda b, i: (b, i, 3)),
        pl.BlockSpec(tile, lambda b, i: (b, prev(b, i), 1)),
        pl.BlockSpec(tile, lambda b, i: (b, i, 1)),
        pl.BlockSpec(tile, lambda b, i: (b, nxt(b, i), 1)),
        pl.BlockSpec(tile, lambda b, i: (b, prev(b, i), 2)),
        pl.BlockSpec(tile, lambda b, i: (b, i, 2)),
        pl.BlockSpec(tile, lambda b, i: (b, nxt(b, i), 2)),
        pl.BlockSpec((1, N_META, BRANCH_W), lambda b, i: (b, mblk, 1)),
        pl.BlockSpec((1, N_META, BRANCH_W), lambda b, i: (b, mblk, 2)),
        pl.BlockSpec(bias.shape, lambda b, i: (0, 0, 0, 0)),
    ]
    return pl.pallas_call(
        functools.partial(_na_kernel, rows=rows),
        grid=(bsz, nb),
        in_specs=in_specs,
        out_specs=pl.BlockSpec(tile, lambda b, i: (b, i, 0)),
        out_shape=jax.ShapeDtypeStruct((bsz, lp, BRANCH_W), F32),
        scratch_shapes=[pltpu.VMEM((3 * ROW_TILE, BRANCH_W), BF16),
                        pltpu.VMEM((3 * ROW_TILE, BRANCH_W), BF16)],
        compiler_params=_cparams(("parallel", "parallel")),
        name="na_attn",
    )(zna, zna, zna, zna, zna, zna, zna, zna, zna, zna, bias)


def _chunk_of(d, n, nchunks):
    back = jnp.where(n <= META_CHUNK, n, nchunks + META_CHUNK - n)
    return jnp.where(d == 0, n, back)


def _order_masks(rev):
    row = lax.broadcasted_iota(jnp.int32, (CHUNK, CHUNK), 0)
    col = lax.broadcasted_iota(jnp.int32, (CHUNK, CHUNK), 1)
    ahead = (col - row) * jnp.where(rev, -1, 1)
    return ahead <= 0, ahead < 0


GLA_PAIRS = GLA_HEADS // 2


def _gla_kernel(zf_ref, zb_ref, gup_ref, gb_ref, of_ref, ob_ref, st_ref, *, nchunks):
    n = pl.program_id(1)

    @pl.when(n == 0)
    def _init():
        st_ref[...] = jnp.zeros_like(st_ref)

    @pl.when(n < META_CHUNK)
    def _padding():
        of_ref[0] = jnp.zeros((CHUNK, BRANCH_W), F32)
        ob_ref[0] = jnp.zeros((CHUNK, BRANCH_W), F32)

    @pl.when(n >= META_CHUNK)
    def _chunk():
        lane = lax.broadcasted_iota(jnp.int32, (1, LANES), 1)
        even = lane < GLA_DK
        row = lax.broadcasted_iota(jnp.int32, (2 * GLA_DV, LANES), 0)
        col = lax.broadcasted_iota(jnp.int32, (2 * GLA_DV, LANES), 1)
        diag_blocks = (row < GLA_DV) == (col < GLA_DK)
        t_row = lax.broadcasted_iota(jnp.int32, (CHUNK, LANES), 0)
        t_col = lax.broadcasted_iota(jnp.int32, (CHUNK, LANES), 1) % CHUNK
        rowid = lax.broadcasted_iota(jnp.int32, (CHUNK, 1), 0)
        live = jnp.logical_or(n > META_CHUNK, rowid >= META_LOCAL0)
        zero_v = jnp.zeros((CHUNK, GLA_DV), BF16)
        dot = functools.partial(jnp.dot, preferred_element_type=F32)

        units = []
        for d, z_ref in enumerate((zf_ref, zb_ref)):
            rev = jnp.logical_and(d == 1, n > META_CHUNK)
            incl, _ = _order_masks(rev)
            incl2 = (t_col - t_row) * jnp.where(rev, -1, 1) <= 0
            z = z_ref[0]
            q = z[:, 0:GLA_KW] * (GLA_DK ** -0.5)
            k = z[:, GLA_KW:2 * GLA_KW]
            v = z[:, 2 * GLA_KW:2 * GLA_KW + BRANCH_W].astype(BF16)
            gd = z[:, 2 * GLA_KW + 2 * BRANCH_W:]
            logits = _mm3(gd, gup_ref[d]) + gb_ref[d]
            gk = (jnp.minimum(logits, 0.0) - jnp.log(1.0 + jnp.exp(-jnp.abs(logits)))) / GLA_NORMALIZER
            gk = jnp.where(live, gk, 0.0)
            bcum = _mm_exact_lhs(incl.astype(BF16), gk)
            tot = jnp.sum(gk, axis=0, keepdims=True)
            e_inv = jnp.exp(-bcum)
            dec = jnp.exp(tot)
            qe = (q * jnp.exp(bcum)).astype(BF16)
            ke = k * e_inv
            kd = (ke * dec).astype(BF16)
            ke = ke.astype(BF16)
            for p in range(GLA_PAIRS):
                sl = slice(p * LANES, (p + 1) * LANES)
                ve = v[:, (2 * p) * GLA_DV:(2 * p + 1) * GLA_DV]
                vo = v[:, (2 * p + 1) * GLA_DV:(2 * p + 2) * GLA_DV]
                units.append(dict(d=d, p=p, sl=sl, incl2=incl2, dec=dec[:, sl], qe=qe[:, sl], ke=ke[:, sl],
                                  kd=kd[:, sl], v2=jnp.concatenate([ve, vo], axis=1),
                                  vbd=jnp.concatenate([jnp.concatenate([ve, zero_v], axis=1),
                                                       jnp.concatenate([zero_v, vo], axis=1)], axis=0)))
        for u in units:
            a = lax.dot_general(u["qe"], _pair_block_diag(u["ke"], even), _NT, preferred_element_type=F32)
            u["a"] = jnp.where(u["incl2"], a, 0.0).astype(BF16)
        for u in units:
            u["st"] = st_ref[u["d"], :, u["sl"]]
            u["o"] = dot(u["a"], u["vbd"]) + lax.dot_general(u["qe"], u["st"].astype(BF16), _NT,
                                                              preferred_element_type=F32)
        for u in units:
            dst = lax.dot_general(u["v2"], u["kd"], _TN, preferred_element_type=F32)
            st_ref[u["d"], :, u["sl"]] = u["st"] * u["dec"] + jnp.where(diag_blocks, dst, 0.0)
        for d, o_ref in enumerate((of_ref, ob_ref)):
            o_ref[0] = jnp.concatenate([u["o"] for u in units if u["d"] == d], axis=-1)


def _gla_branch(zgla, gup_pad, gb):
    bsz, lp, zc = zgla.shape
    nchunks = lp // CHUNK
    back = lambda n: _chunk_of(1, n, nchunks)
    full = lambda a: pl.BlockSpec(a.shape, lambda b, n: (0,) * a.ndim)
    out_f = pl.BlockSpec((1, CHUNK, BRANCH_W), lambda b, n: (b, n, 0))
    out_b = pl.BlockSpec((1, CHUNK, BRANCH_W), lambda b, n: (b, back(n), 0))
    sd = jax.ShapeDtypeStruct((bsz, lp, BRANCH_W), F32)
    return pl.pallas_call(
        functools.partial(_gla_kernel, nchunks=nchunks),
        grid=(bsz, nchunks),
        in_specs=[
            pl.BlockSpec((1, CHUNK, zc), lambda b, n: (b, n, 0)),
            pl.BlockSpec((1, CHUNK, zc), lambda b, n: (b, back(n), 0)),
            full(gup_pad), full(gb),
        ],
        out_specs=[out_f, out_b],
        out_shape=[sd, sd],
        scratch_shapes=[pltpu.VMEM((2, 2 * GLA_DV, GLA_KW), F32)],
        compiler_params=_cparams(("parallel", "arbitrary")),
        name="gla_scan",
    )(zgla, zgla, gup_pad, gb)


HALO = 8


def _rw_prep_kernel(*refs, mix):
    if mix:
        (z_ref, zp_ref, zn_ref, vf_ref, conv_ref, w0_ref, wup_ref, a0_ref, aup_ref, kk_ref, ka_ref,
         rk_ref, ones_ref, v0_ref, vdn_ref, vup_ref,
         r_o, k_o, v_o, kkn_o, lw_o, al_o, bonus_o, sc_ref) = refs
    else:
        (z_ref, zp_ref, zn_ref, conv_ref, w0_ref, wup_ref, a0_ref, aup_ref, kk_ref, ka_ref,
         rk_ref, ones_ref,
         r_o, k_o, v_o, kkn_o, lw_o, al_o, bonus_o, sc_ref) = refs
    i = pl.program_id(1)
    last = pl.num_programs(1) - 1
    nsh = 3 * BRANCH_W
    sc_ref[HALO:HALO + ROW_TILE, 0:nsh] = z_ref[0, :, 0:nsh]
    sc_ref[HALO:HALO + ROW_TILE, nsh:] = z_ref[0, :, nsh + BRANCH_W:]
    sc_ref[0:HALO, 0:nsh] = zp_ref[0, :, 0:nsh]
    sc_ref[0:HALO, nsh:] = zp_ref[0, :, nsh + BRANCH_W:]
    nxt_keep = jnp.where(i == last, 0.0, 1.0)
    sc_ref[HALO + ROW_TILE:, 0:nsh] = zn_ref[0, :, 0:nsh] * nxt_keep
    sc_ref[HALO + ROW_TILE:, nsh:] = zn_ref[0, :, nsh + BRANCH_W:] * nxt_keep
    zm = sc_ref[HALO - 1:HALO - 1 + ROW_TILE, :]
    z0 = sc_ref[HALO:HALO + ROW_TILE, :]
    zp = sc_ref[HALO + 1:HALO + 1 + ROW_TILE, :]
    cw = conv_ref[...]
    zs = zm * cw[0:1] + z0 * cw[1:2] + zp * cw[2:3]
    r = zs[:, 0:BRANCH_W]
    k = zs[:, BRANCH_W:2 * BRANCH_W]
    v = zs[:, 2 * BRANCH_W:nsh]
    wd = zs[:, nsh:nsh + 2 * RW_W_LORA]
    ad = zs[:, nsh + 2 * RW_W_LORA:]
    rowid = lax.broadcasted_iota(jnp.int32, (ROW_TILE, 1), 0)
    live = jnp.logical_or(i > 0, rowid >= META_ROW0)
    twd = jnp.tanh(wd)
    als = []
    for d in range(2):
        wl = _mm3(twd, wup_ref[d]) + w0_ref[d]
        lw = -jnp.exp(-_softplus(-wl) - 0.5)
        lw_o[d, 0] = jnp.where(live, lw, 0.0)
        al = _sigmoid(_mm3(ad, aup_ref[d]) + a0_ref[d])
        al_o[d, 0] = al
        als.append(al)
    kkr = k * kk_ref[...]
    nrm2 = _mm_exact_rhs(kkr * kkr, ones_ref[...])
    kkn = kkr / jnp.maximum(jnp.sqrt(nrm2), 1e-12)
    ka = ka_ref[...]
    kmod0 = k * (1.0 + (als[0] - 1.0) * ka)
    kmod1 = k * (1.0 + (als[1] - 1.0) * ka)
    if mix:
        gate = _sigmoid(v0_ref[...] + _mm3(_mm3(v, vdn_ref[...]), vup_ref[...]))
        v = v + (vf_ref[0] - v) * gate
    kb = 0.5 * (kmod0 + kmod1)
    bsum = _mm_exact_rhs(r * kb * rk_ref[...], ones_ref[...])
    bonus_o[0] = bsum * v
    r_o[0] = r
    k_o[0] = jnp.where(live, k, 0.0)
    v_o[0] = jnp.where(live, v, 0.0)
    kkn_o[0] = jnp.where(live, kkn, 0.0)


def _rw_prep(zrw, vfirst, conv_w, w0, wup_pad, a0, aup_pad, k_k, k_a, r_k, ones_bd, vmix):
    bsz, lp, zc = zrw.shape
    nb = lp // ROW_TILE
    nhalo = lp // HALO
    per_tile = ROW_TILE // HALO
    mix = vmix is not None
    tile = pl.BlockSpec((1, ROW_TILE, BRANCH_W), lambda b, i: (b, i, 0))
    dtile = pl.BlockSpec((2, 1, ROW_TILE, BRANCH_W), lambda b, i: (0, b, i, 0))
    full = lambda a: pl.BlockSpec(a.shape, lambda b, i: (0,) * a.ndim)
    in_specs = [
        pl.BlockSpec((1, ROW_TILE, zc), lambda b, i: (b, i, 0)),
        pl.BlockSpec((1, HALO, zc), lambda b, i: (b, jnp.maximum(i * per_tile - 1, 0), 0)),
        pl.BlockSpec((1, HALO, zc), lambda b, i: (b, jnp.minimum((i + 1) * per_tile, nhalo - 1), 0)),
    ]
    args = [zrw, zrw, zrw]
    if mix:
        in_specs.append(tile)
        args.append(vfirst)
    params = [conv_w, w0, wup_pad, a0, aup_pad, k_k, k_a, r_k, ones_bd]
    if mix:
        params += list(vmix)
    in_specs += [full(p) for p in params]
    args += params
    sd = jax.ShapeDtypeStruct((bsz, lp, BRANCH_W), F32)
    sd2 = jax.ShapeDtypeStruct((2, bsz, lp, BRANCH_W), F32)
    return pl.pallas_call(
        functools.partial(_rw_prep_kernel, mix=mix),
        grid=(bsz, nb),
        in_specs=in_specs,
        out_specs=[tile, tile, tile, tile, dtile, dtile, tile],
        out_shape=[sd, sd, sd, sd, sd2, sd2, sd],
        scratch_shapes=[pltpu.VMEM((ROW_TILE + 2 * HALO, 3 * BRANCH_W + RW_LORA_COLS), F32)],
        compiler_params=_cparams(("parallel", "parallel")),
        name="rw_prep",
    )(*args)


RW_PAIRS = RW_HEADS // 2


def _pair_block_diag(x, even):
    z = jnp.zeros_like(x)
    return jnp.concatenate([jnp.where(even, x, z), jnp.where(even, z, x)], axis=0)


def _rw_scan_kernel(rf_ref, kf_ref, vf_ref, kkf_ref, lwf_ref, alf_ref,
                    rb_ref, kb_ref, vb_ref, kkb_ref, lwb_ref, alb_ref, ka_ref,
                    yf_ref, yb_ref, ht_ref, *, nchunks):
    n = pl.program_id(1)

    @pl.when(n == 0)
    def _init():
        ht_ref[...] = jnp.zeros_like(ht_ref)

    @pl.when(n < META_CHUNK)
    def _padding():
        yf_ref[0] = jnp.zeros((CHUNK, BRANCH_W), F32)
        yb_ref[0] = jnp.zeros((CHUNK, BRANCH_W), F32)

    @pl.when(n >= META_CHUNK)
    def _chunk():
        lane = lax.broadcasted_iota(jnp.int32, (1, LANES), 1)
        even = lane < RW_N
        row = lax.broadcasted_iota(jnp.int32, (2 * RW_N, LANES), 0)
        col = lax.broadcasted_iota(jnp.int32, (2 * RW_N, LANES), 1)
        diag_blocks = (row < RW_N) == (col < RW_N)
        t_row = lax.broadcasted_iota(jnp.int32, (CHUNK, LANES), 0)
        t_col = lax.broadcasted_iota(jnp.int32, (CHUNK, LANES), 1) % CHUNK
        eye2 = (t_row == t_col).astype(F32)
        ka = ka_ref[...]
        bd = functools.partial(_pair_block_diag, even=even)

        units = []
        for d, refs in enumerate(((rf_ref, kf_ref, vf_ref, kkf_ref, lwf_ref, alf_ref),
                                  (rb_ref, kb_ref, vb_ref, kkb_ref, lwb_ref, alb_ref))):
            r_ref, k_ref, v_ref, kk_ref, lw_ref, al_ref = refs
            rev = jnp.logical_and(d == 1, n > META_CHUNK)
            incl_m, _ = _order_masks(rev)
            ahead2 = (t_col - t_row) * jnp.where(rev, -1, 1)
            lw = lw_ref[0, 0]
            al = al_ref[0, 0]
            kk = kk_ref[0]
            incl = _mm_exact_lhs(incl_m.astype(BF16), lw)
            tot = jnp.sum(lw, axis=0, keepdims=True)
            e_inv = jnp.exp(-incl)
            ptot = jnp.exp(tot)
            kmod = k_ref[0] * (1.0 + (al - 1.0) * ka)
            bt = (kk * al) * e_inv
            kt = kmod * e_inv
            common = dict(
                d=d, incl2=ahead2 <= 0, strict2=ahead2 < 0, ptot=ptot,
                at=(-kk * jnp.exp(incl - lw)).astype(BF16),
                rt=(r_ref[0] * jnp.exp(incl)).astype(BF16),
                bt=bt.astype(BF16), kt=kt.astype(BF16),
                bp=(bt * ptot).astype(BF16), kp=(kt * ptot).astype(BF16),
                v=v_ref[0].astype(BF16))
            for p in range(RW_PAIRS):
                units.append(dict(common, p=p, sl=slice(p * LANES, (p + 1) * LANES)))

        dot = functools.partial(jnp.dot, preferred_element_type=F32)
        for u in units:
            sl = u["sl"]
            lhs = jnp.concatenate([u["at"][:, sl], u["rt"][:, sl]], axis=0)
            rhs = jnp.concatenate([bd(u["bt"][:, sl]), bd(u["kt"][:, sl])], axis=0)
            g = lax.dot_general(lhs, rhs, _NT, preferred_element_type=F32)
            u["n"] = jnp.where(u["strict2"], g[:CHUNK, :LANES], 0.0)
            a_ak = jnp.where(u["strict2"], g[:CHUNK, LANES:], 0.0)
            u["a_r"] = jnp.where(jnp.concatenate([u["incl2"], u["incl2"]], axis=1), g[CHUNK:], 0.0).astype(BF16)
            u["a_ak"] = a_ak.astype(BF16)
        for u in units:
            u["av"] = dot(u["a_ak"], bd(u["v"][:, u["sl"]]))
        steps = int(math.log2(CHUNK))
        for u in units:
            u["t"] = eye2 + u["n"]
        for s in range(steps):
            for u in units:
                nb = u["n"].astype(BF16)
                if s == 0:
                    u["n"] = dot(nb, bd(nb))
                elif s < steps - 1:
                    zz = dot(nb, jnp.concatenate([bd(u["t"].astype(BF16)), bd(nb)], axis=1))
                    u["t"] = u["t"] + zz[:, :LANES]
                    u["n"] = zz[:, LANES:]
                else:
                    u["t"] = u["t"] + dot(nb, bd(u["t"].astype(BF16)))
        for u in units:
            rhs = jnp.concatenate([bd(u["at"][:, u["sl"]]), bd(u["av"].astype(BF16))], axis=1)
            u["wx"] = dot(u["t"].astype(BF16), rhs)
        for u in units:
            sl = u["sl"]
            u["h"] = ht_ref[u["d"], :, sl]
            lhs = jnp.concatenate([u["wx"][:, :LANES].astype(BF16), u["rt"][:, sl]], axis=0)
            ru = lax.dot_general(lhs, u["h"].astype(BF16), _NT, preferred_element_type=F32)
            u["u"] = (ru[:CHUNK] + u["wx"][:, LANES:]).astype(BF16)
            u["rh"] = ru[CHUNK:]
        for u in units:
            sl = u["sl"]
            rhs = jnp.concatenate([bd(u["u"]), bd(u["v"][:, sl])], axis=0)
            u["y"] = u["rh"] + dot(u["a_r"], rhs)
            dh = lax.dot_general(jnp.concatenate([u["u"], u["v"][:, sl]], axis=0),
                                 jnp.concatenate([u["bp"][:, sl], u["kp"][:, sl]], axis=0),
                                 _TN, preferred_element_type=F32)
            ht_ref[u["d"], :, sl] = u["h"] * u["ptot"][:, sl] + jnp.where(diag_blocks, dh, 0.0)
        for d, y_ref in enumerate((yf_ref, yb_ref)):
            y_ref[0] = jnp.concatenate([u["y"] for u in units if u["d"] == d], axis=-1)


def _rw_scan(r, k, v, kk, lw, al, k_a):
    bsz, lp, _ = r.shape
    nchunks = lp // CHUNK
    back = lambda n: _chunk_of(1, n, nchunks)
    tok_f = pl.BlockSpec((1, CHUNK, BRANCH_W), lambda b, n: (b, n, 0))
    tok_b = pl.BlockSpec((1, CHUNK, BRANCH_W), lambda b, n: (b, back(n), 0))
    dir_f = pl.BlockSpec((1, 1, CHUNK, BRANCH_W), lambda b, n: (0, b, n, 0))
    dir_b = pl.BlockSpec((1, 1, CHUNK, BRANCH_W), lambda b, n: (1, b, back(n), 0))
    sd = jax.ShapeDtypeStruct((bsz, lp, BRANCH_W), F32)
    yf, yb = pl.pallas_call(
        functools.partial(_rw_scan_kernel, nchunks=nchunks),
        grid=(bsz, nchunks),
        in_specs=[tok_f, tok_f, tok_f, tok_f, dir_f, dir_f,
                  tok_b, tok_b, tok_b, tok_b, dir_b, dir_b,
                  pl.BlockSpec((1, BRANCH_W), lambda b, n: (0, 0))],
        out_specs=[tok_f, tok_b],
        out_shape=[sd, sd],
        scratch_shapes=[pltpu.VMEM((2, 2 * RW_N, BRANCH_W), F32)],
        compiler_params=_cparams(("parallel", "arbitrary")),
        name="rw_scan",
    )(r, k, v, kk, lw, al, r, k, v, kk, lw, al, k_a)
    return yf, yb


def _out_kernel(h_ref, ona_ref, ogf_ref, ogb_ref, gg_ref, yf_ref, yb_ref, bonus_ref, gate_ref, wout_ref, gng_ref,
                avg_ref, lng_ref, lnb_ref, fin_ref, o_ref, *, final, tile_off):
    i = pl.program_id(1) + tile_off
    og = ogf_ref[0] + ogb_ref[0]
    parts = []
    for h in range(GLA_HEADS):
        oh = og[:, h * GLA_DV:(h + 1) * GLA_DV]
        ms = jnp.mean(oh * oh, axis=-1, keepdims=True)
        parts.append(oh * lax.rsqrt(ms + GLA_NORM_EPS) * gng_ref[...])
    o_gla = jnp.concatenate(parts, axis=-1) * _silu(gg_ref[0])
    y = yf_ref[0] + yb_ref[0]
    mu = _mm_exact_rhs(y, avg_ref[...])
    yc = y - mu
    var = _mm_exact_rhs(yc * yc, avg_ref[...])
    yn = yc * lax.rsqrt(var + RW_GN_EPS) * lng_ref[...] + lnb_ref[...]
    o_rw = (yn + bonus_ref[0]) * _silu(gate_ref[0])
    cat = jnp.concatenate([ona_ref[0].astype(BF16), o_gla.astype(BF16), o_rw.astype(BF16)], axis=-1)
    hn = h_ref[0] + jnp.dot(cat, wout_ref[...], preferred_element_type=F32)
    if final:
        ms = jnp.mean(hn * hn, axis=-1, keepdims=True)
        o_ref[0] = hn * lax.rsqrt(ms + NORM_EPS) * fin_ref[...]
    else:
        rowid = lax.broadcasted_iota(jnp.int32, (ROW_TILE, 1), 0)
        live = jnp.logical_or(i > 0, rowid >= META_ROW0)
        o_ref[0] = jnp.where(live, hn, 0.0)


def _out_proj(hp, ona, ogla, zgla, yrw, bonus, zrw, wout, gla_norm_g, avg_bd, ln_g, ln_b, fin_g, final):
    bsz, lp, _ = hp.shape
    nb = lp // ROW_TILE
    off = 1 if final else 0
    nt = nb - off
    tile = lambda w, cb: pl.BlockSpec((1, ROW_TILE, w), lambda b, i: (b, i + off, cb))
    full = lambda a: pl.BlockSpec(a.shape, lambda b, i: (0,) * a.ndim)
    gla_gate_blk = (2 * GLA_KW + BRANCH_W) // BRANCH_W
    rw_gate_blk = 3
    out_rows = lp - off * ROW_TILE
    return pl.pallas_call(
        functools.partial(_out_kernel, final=final, tile_off=off),
        grid=(bsz, nt),
        in_specs=[
            tile(D_MODEL, 0), tile(BRANCH_W, 0), tile(BRANCH_W, 0), tile(BRANCH_W, 0),
            tile(BRANCH_W, gla_gate_blk),
            tile(BRANCH_W, 0), tile(BRANCH_W, 0), tile(BRANCH_W, 0), tile(BRANCH_W, rw_gate_blk),
            full(wout), full(gla_norm_g), full(avg_bd), full(ln_g), full(ln_b), full(fin_g),
        ],
        out_specs=pl.BlockSpec((1, ROW_TILE, D_MODEL), lambda b, i: (b, i, 0)),
        out_shape=jax.ShapeDtypeStruct((bsz, out_rows, D_MODEL), F32),
        compiler_params=_cparams(("parallel", "parallel")),
        name="out_proj",
    )(hp, ona, ogla[0], ogla[1], zgla, yrw[0], yrw[1], bonus, zrw, wout, gla_norm_g, avg_bd, ln_g, ln_b, fin_g)


def _block_diag(n, blk, val):
    idx = np.arange(n) // blk
    return jnp.asarray((idx[:, None] == idx[None, :]).astype(np.float32) * val)


def _pad_rows_per_dir(w, rows_total):
    _, r, c = w.shape
    out = jnp.zeros((2, rows_total, c), w.dtype)
    out = out.at[0, 0:r].set(w[0])
    out = out.at[1, r:2 * r].set(w[1])
    return out


def kernel(x, meta, norm_g, w_in, w_out, na_rpb, gla_g_up, gla_g_b, gla_norm_g, rw_conv, rw_w0, rw_w_up,
           rw_a0, rw_a_up, rw_k_k, rw_k_a, rw_r_k, rw_ln_g, rw_ln_b, rw_v0, rw_v_down, rw_v_up, final_norm_g):
    bsz, t, dm = x.shape
    assert dm == D_MODEL and t % ROW_TILE == 0 and t // GRID_W >= WIN_R
    depth = w_in.shape[0]
    lp = t + PAD
    rows = t // GRID_W
    hp = jnp.concatenate([
        jnp.zeros((bsz, META_ROW0, dm), x.dtype),
        jnp.broadcast_to(meta[None].astype(x.dtype), (bsz, N_META, dm)),
        x], axis=1)
    ones_bd = _block_diag(BRANCH_W, RW_N, 1.0).astype(BF16)
    avg_bd = _block_diag(BRANCH_W, RW_N, 1.0 / RW_N).astype(BF16)
    nsh = 3 * BRANCH_W
    vfirst = None
    out = None
    for l in range(depth):
        w = w_in[l]
        wna = w[:, :NA_COLS].astype(BF16)
        wg = w[:, NA_COLS:NA_COLS + GLA_COLS]
        wgla = jnp.pad(wg, ((0, 0), (0, GLA_ZCOLS - GLA_COLS))).astype(BF16)
        wr = w[:, NA_COLS + GLA_COLS:]
        wrw = jnp.concatenate([wr[:, :nsh], wr[:, RW_SHIFT_COLS:], wr[:, nsh:RW_SHIFT_COLS]], axis=1).astype(BF16)
        conv_w = rw_conv[l]
        zna, zgla, zrw = _in_proj(hp.reshape(bsz * lp, dm), norm_g[l][None], wna, wgla, wrw)
        zna = zna.reshape(bsz, lp, -1)
        zgla = zgla.reshape(bsz, lp, -1)
        zrw = zrw.reshape(bsz, lp, -1)

        ona = _na_branch(zna, _na_bias_table(na_rpb[l]), rows)

        gup_pad = _pad_rows_per_dir(gla_g_up[l], LANES)
        ogla = _gla_branch(zgla, gup_pad, gla_g_b[l][:, None, :])

        wup_pad = _pad_rows_per_dir(rw_w_up[l], 2 * RW_W_LORA)
        aup_pad = _pad_rows_per_dir(rw_a_up[l], 2 * RW_A_LORA)
        if l == 0:
            vmix = None
        else:
            vdn = jnp.pad(rw_v_down[l - 1], ((0, 0), (0, LANES - RW_V_LORA)))
            vup = jnp.pad(rw_v_up[l - 1], ((0, LANES - RW_V_LORA), (0, 0)))
            vmix = (rw_v0[l - 1][None], vdn, vup)
        r_, k_, v_, kk_, lw_, al_, bonus = _rw_prep(
            zrw, vfirst, conv_w, rw_w0[l][:, None, :], wup_pad, rw_a0[l][:, None, :], aup_pad,
            rw_k_k[l][None], rw_k_a[l][None], rw_r_k[l][None], ones_bd, vmix)
        if l == 0:
            vfirst = v_
        yrw = _rw_scan(r_, k_, v_, kk_, lw_, al_, rw_k_a[l][None])

        final = l == depth - 1
        res = _out_proj(hp, ona, ogla, zgla, yrw, bonus, zrw, w_out[l].astype(BF16), gla_norm_g[l][None],
                        avg_bd, rw_ln_g[l][None], rw_ln_b[l][None], final_norm_g[None], final)
        if final:
            out = res
        else:
            hp = res
    return out
```

```python
import functools
import math

import numpy as np
import jax
import jax.numpy as jnp
from jax import lax
from jax.experimental import pallas as pl
from jax.experimental.pallas import tpu as pltpu

F32 = jnp.float32
BF16 = jnp.bfloat16

D_MODEL = 1024
N_META = 16
GRID_W = 64
BRANCH_W = D_MODEL // 2
D_MIX = 3 * BRANCH_W
NORM_EPS = 1e-6
NA_HD = 64
NA_HEADS = BRANCH_W // NA_HD
WIN_R = 8
WIN_C = 16
GLA_HEADS = 4
GLA_DV = BRANCH_W // GLA_HEADS
GLA_DK = GLA_DV // 2
GLA_KW = GLA_HEADS * GLA_DK
GLA_LR = 16
GLA_NORMALIZER = 16.0
GLA_NORM_EPS = 1e-5
RW_N = 64
RW_HEADS = BRANCH_W // RW_N
RW_W_LORA = 64
RW_A_LORA = 64
RW_V_LORA = 32
RW_GN_EPS = 64e-5
NA_COLS = 4 * BRANCH_W
GLA_COLS = 2 * GLA_KW + 2 * BRANCH_W + 2 * GLA_LR
RW_SHIFT_COLS = 3 * BRANCH_W + 2 * RW_W_LORA + 2 * RW_A_LORA
RW_COLS = RW_SHIFT_COLS + BRANCH_W

LANES = 128
ROW_TILE = 512
CHUNK = 64
PAD = ROW_TILE
META_ROW0 = PAD - N_META
META_CHUNK = META_ROW0 // CHUNK
META_LOCAL0 = META_ROW0 - META_CHUNK * CHUNK
GLA_ZCOLS = 2 * GLA_KW + 2 * BRANCH_W + LANES
RW_LORA_COLS = 2 * RW_W_LORA + 2 * RW_A_LORA
NEG_BIG = -1e30
VMEM_LIMIT = 56 * 1024 * 1024

_NT = (((1,), (1,)), ((), ()))
_TN = (((0,), (0,)), ((), ()))


def _mm(a, b):
    return jnp.dot(a.astype(BF16), b.astype(BF16), preferred_element_type=F32)


def _mm_nt(a, b):
    return lax.dot_general(a.astype(BF16), b.astype(BF16), _NT, preferred_element_type=F32)


def _mm_tn(a, b):
    return lax.dot_general(a.astype(BF16), b.astype(BF16), _TN, preferred_element_type=F32)


def _split2(x):
    hi = x.astype(BF16)
    lo = (x - hi.astype(F32)).astype(BF16)
    return hi, lo


def _split3(x):
    hi = x.astype(BF16)
    r = x - hi.astype(F32)
    mid = r.astype(BF16)
    lo = (r - mid.astype(F32)).astype(BF16)
    return hi, mid, lo


def _mm3(a, b):
    ah, al = _split2(a)
    bh, bl = _split2(b)
    d = functools.partial(jnp.dot, preferred_element_type=F32)
    return d(ah, bh) + (d(ah, bl) + d(al, bh))


def _mm_exact_lhs(a_bf16, x):
    xh, xl = _split2(x)
    d = functools.partial(jnp.dot, preferred_element_type=F32)
    return d(a_bf16, xh) + d(a_bf16, xl)


def _mm_exact_rhs(x, b_bf16):
    xh, xl = _split2(x)
    d = functools.partial(jnp.dot, preferred_element_type=F32)
    return d(xh, b_bf16) + d(xl, b_bf16)


def _sigmoid(x):
    return 1.0 / (1.0 + jnp.exp(-x))


def _silu(x):
    return x * _sigmoid(x)


def _softplus(x):
    return jnp.maximum(x, 0.0) + jnp.log(1.0 + jnp.exp(-jnp.abs(x)))


def _cparams(sem):
    return pltpu.CompilerParams(dimension_semantics=sem, vmem_limit_bytes=VMEM_LIMIT)


def _inproj_kernel(h_ref, g_ref, wna_ref, wgla_ref, wrw_ref, zna_ref, zgla_ref, zrw_ref):
    h = h_ref[...]
    ms = jnp.mean(h * h, axis=-1, keepdims=True)
    hn = (h * lax.rsqrt(ms + NORM_EPS) * g_ref[...]).astype(BF16)
    zna_ref[...] = jnp.dot(hn, wna_ref[...], preferred_element_type=F32)
    zgla_ref[...] = jnp.dot(hn, wgla_ref[...], preferred_element_type=F32)
    zrw_ref[...] = jnp.dot(hn, wrw_ref[...], preferred_element_type=F32)


def _in_proj(h2d, norm_g, wna, wgla, wrw):
    n = h2d.shape[0]
    tm = ROW_TILE // 2
    const = lambda i: (0, 0)
    row = lambda i: (i, 0)
    return pl.pallas_call(
        _inproj_kernel,
        grid=(n // tm,),
        in_specs=[
            pl.BlockSpec((tm, D_MODEL), row),
            pl.BlockSpec((1, D_MODEL), const),
            pl.BlockSpec(wna.shape, const),
            pl.BlockSpec(wgla.shape, const),
            pl.BlockSpec(wrw.shape, const),
        ],
        out_specs=[
            pl.BlockSpec((tm, wna.shape[1]), row),
            pl.BlockSpec((tm, wgla.shape[1]), row),
            pl.BlockSpec((tm, wrw.shape[1]), row),
        ],
        out_shape=[
            jax.ShapeDtypeStruct((n, wna.shape[1]), F32),
            jax.ShapeDtypeStruct((n, wgla.shape[1]), F32),
            jax.ShapeDtypeStruct((n, wrw.shape[1]), F32),
        ],
        compiler_params=_cparams(("parallel",)),
        name="in_proj",
    )(h2d, norm_g, wna, wgla, wrw)


NA_QROWS = ROW_TILE // GRID_W


def _na_kernel(q_ref, g_ref, kp_ref, kc_ref, kn_ref, vp_ref, vc_ref, vn_ref, km_ref, vm_ref,
               bias_ref, o_ref, ks_ref, vs_ref, *, rows):
    i = pl.program_id(1)
    scale = NA_HD ** -0.5
    km = km_ref[0].astype(BF16)
    vm = vm_ref[0].astype(BF16)

    @pl.when(i == 0)
    def _meta_tile():
        o_ref[0, 0:META_ROW0, :] = jnp.zeros((META_ROW0, BRANCH_W), F32)
        qm = (q_ref[0, META_ROW0:PAD, :] * scale).astype(BF16)
        outs = []
        for h in range(NA_HEADS):
            sl = slice(h * NA_HD, (h + 1) * NA_HD)
            s = lax.dot_general(qm[:, sl], km[:, sl], _NT, preferred_element_type=F32)
            m = jnp.max(s, axis=-1, keepdims=True)
            p = jnp.exp(s - m)
            den = jnp.sum(p, axis=-1, keepdims=True)
            outs.append(jnp.dot(p.astype(BF16), vm[:, sl], preferred_element_type=F32) / den)
        om = jnp.concatenate(outs, axis=-1)
        o_ref[0, META_ROW0:PAD, :] = om * _silu(g_ref[0, META_ROW0:PAD, :])

    @pl.when(i > 0)
    def _real_tile():
        ks_ref[0:ROW_TILE, :] = kp_ref[0].astype(BF16)
        ks_ref[ROW_TILE:2 * ROW_TILE, :] = kc_ref[0].astype(BF16)
        ks_ref[2 * ROW_TILE:3 * ROW_TILE, :] = kn_ref[0].astype(BF16)
        vs_ref[0:ROW_TILE, :] = vp_ref[0].astype(BF16)
        vs_ref[ROW_TILE:2 * ROW_TILE, :] = vc_ref[0].astype(BF16)
        vs_ref[2 * ROW_TILE:3 * ROW_TILE, :] = vn_ref[0].astype(BF16)
        r0 = (i - 1) * NA_QROWS

        def row_body(j, carry):
            r = r0 + j
            rs = jnp.clip(r - WIN_R // 2, 0, rows - WIN_R)
            start = pl.multiple_of((rs - r0 + NA_QROWS) * GRID_W, GRID_W)
            pat = r - rs
            qoff = pl.multiple_of(j * GRID_W, GRID_W)
            qj = (q_ref[0, pl.ds(qoff, GRID_W), :] * scale).astype(BF16)
            kw = ks_ref[pl.ds(start, WIN_R * GRID_W), :]
            vw = vs_ref[pl.ds(start, WIN_R * GRID_W), :]
            heads = [slice(h * NA_HD, (h + 1) * NA_HD) for h in range(NA_HEADS)]
            scores = [(lax.dot_general(qj[:, sl], kw[:, sl], _NT, preferred_element_type=F32) + bias_ref[h, pat],
                       lax.dot_general(qj[:, sl], km[:, sl], _NT, preferred_element_type=F32))
                      for h, sl in enumerate(heads)]
            probs = []
            for s, sm in scores:
                m = jnp.maximum(jnp.max(s, axis=-1, keepdims=True),
                                jnp.max(sm, axis=-1, keepdims=True))
                p = jnp.exp(s - m)
                pm = jnp.exp(sm - m)
                den = jnp.sum(p, axis=-1, keepdims=True) + jnp.sum(pm, axis=-1, keepdims=True)
                probs.append((p.astype(BF16), pm.astype(BF16), den))
            outs = [(jnp.dot(p, vw[:, sl], preferred_element_type=F32)
                     + jnp.dot(pm, vm[:, sl], preferred_element_type=F32)) / den
                    for (p, pm, den), sl in zip(probs, heads)]
            oj = jnp.concatenate(outs, axis=-1)
            o_ref[0, pl.ds(qoff, GRID_W), :] = oj * _silu(g_ref[0, pl.ds(qoff, GRID_W), :])
            return carry

        lax.fori_loop(0, NA_QROWS, row_body, 0, unroll=2)


def _na_bias_table(rpb):
    cols = np.arange(GRID_W)
    cstart = np.clip(cols - WIN_C // 2, 0, GRID_W - WIN_C)
    kc = np.arange(GRID_W)
    inwin = (kc[None, :] >= cstart[:, None]) & (kc[None, :] < cstart[:, None] + WIN_C)
    dc = np.clip(kc[None, :] - cols[:, None] + (WIN_C - 1), 0, 2 * WIN_C - 2)
    pats = np.arange(WIN_R)
    jj = np.arange(WIN_R)
    dr = jj[None, :] - pats[:, None] + (WIN_R - 1)
    b = rpb[:, dr][:, :, :, dc]
    b = jnp.where(inwin[None, None, None], b, NEG_BIG)
    b = b.transpose(0, 1, 3, 2, 4).reshape(NA_HEADS, WIN_R, GRID_W, WIN_R * GRID_W)
    return b.astype(F32)


def _na_branch(zna, bias, rows):
    bsz, lp, _ = zna.shape
    nb = lp // ROW_TILE
    mblk = META_ROW0 // N_META
    tile = (1, ROW_TILE, BRANCH_W)
    prev = lambda b, i: jnp.maximum(i - 1, 0)
    nxt = lambda b, i: jnp.minimum(i + 1, nb - 1)
    in_specs = [
        pl.BlockSpec(tile, lambda b, i: (b, i, 0)),
        pl.BlockSpec(tile, lambda b, i: (b, i, 3)),
        pl.BlockSpec(tile, lambda b, i: (b, prev(b, i), 1)),
        pl.BlockSpec(tile, lambda b, i: (b, i, 1)),
        pl.BlockSpec(tile, lambda b, i: (b, nxt(b, i), 1)),
        pl.BlockSpec(tile, lambda b, i: (b, prev(b, i), 2)),
        pl.BlockSpec(tile, lambda b, i: (b, i, 2)),
        pl.BlockSpec(tile, lambda b, i: (b, nxt(b, i), 2)),
        pl.BlockSpec((1, N_META, BRANCH_W), lambda b, i: (b, mblk, 1)),
        pl.BlockSpec((1, N_META, BRANCH_W), lambda b, i: (b, mblk, 2)),
        pl.BlockSpec(bias.shape, lambda b, i: (0, 0, 0, 0)),
    ]
    return pl.pallas_call(
        functools.partial(_na_kernel, rows=rows),
        grid=(bsz, nb),
        in_specs=in_specs,
        out_specs=pl.BlockSpec(tile, lambda b, i: (b, i, 0)),
        out_shape=jax.ShapeDtypeStruct((bsz, lp, BRANCH_W), F32),
        scratch_shapes=[pltpu.VMEM((3 * ROW_TILE, BRANCH_W), BF16),
                        pltpu.VMEM((3 * ROW_TILE, BRANCH_W), BF16)],
        compiler_params=_cparams(("parallel", "parallel")),
        name="na_attn",
    )(zna, zna, zna, zna, zna, zna, zna, zna, zna, zna, bias)


def _chunk_of(d, n, nchunks):
    back = jnp.where(n <= META_CHUNK, n, nchunks + META_CHUNK - n)
    return jnp.where(d == 0, n, back)


def _order_masks(rev):
    row = lax.broadcasted_iota(jnp.int32, (CHUNK, CHUNK), 0)
    col = lax.broadcasted_iota(jnp.int32, (CHUNK, CHUNK), 1)
    ahead = (col - row) * jnp.where(rev, -1, 1)
    return ahead <= 0, ahead < 0


GLA_PAIRS = GLA_HEADS // 2


def _gla_kernel(zf_ref, zb_ref, gup_ref, gb_ref, of_ref, ob_ref, st_ref, *, nchunks):
    n = pl.program_id(1)

    @pl.when(n == 0)
    def _init():
        st_ref[...] = jnp.zeros_like(st_ref)

    @pl.when(n < META_CHUNK)
    def _padding():
        of_ref[0] = jnp.zeros((CHUNK, BRANCH_W), F32)
        ob_ref[0] = jnp.zeros((CHUNK, BRANCH_W), F32)

    @pl.when(n >= META_CHUNK)
    def _chunk():
        lane = lax.broadcasted_iota(jnp.int32, (1, LANES), 1)
        even = lane < GLA_DK
        row = lax.broadcasted_iota(jnp.int32, (2 * GLA_DV, LANES), 0)
        col = lax.broadcasted_iota(jnp.int32, (2 * GLA_DV, LANES), 1)
        diag_blocks = (row < GLA_DV) == (col < GLA_DK)
        t_row = lax.broadcasted_iota(jnp.int32, (CHUNK, LANES), 0)
        t_col = lax.broadcasted_iota(jnp.int32, (CHUNK, LANES), 1) % CHUNK
        rowid = lax.broadcasted_iota(jnp.int32, (CHUNK, 1), 0)
        live = jnp.logical_or(n > META_CHUNK, rowid >= META_LOCAL0)
        zero_v = jnp.zeros((CHUNK, GLA_DV), BF16)
        dot = functools.partial(jnp.dot, preferred_element_type=F32)

        units = []
        for d, z_ref in enumerate((zf_ref, zb_ref)):
            rev = jnp.logical_and(d == 1, n > META_CHUNK)
            incl, _ = _order_masks(rev)
            incl2 = (t_col - t_row) * jnp.where(rev, -1, 1) <= 0
            z = z_ref[0]
            q = z[:, 0:GLA_KW] * (GLA_DK ** -0.5)
            k = z[:, GLA_KW:2 * GLA_KW]
            v = z[:, 2 * GLA_KW:2 * GLA_KW + BRANCH_W].astype(BF16)
            gd = z[:, 2 * GLA_KW + 2 * BRANCH_W:]
            logits = _mm3(gd, gup_ref[d]) + gb_ref[d]
            gk = (jnp.minimum(logits, 0.0) - jnp.log(1.0 + jnp.exp(-jnp.abs(logits)))) / GLA_NORMALIZER
            gk = jnp.where(live, gk, 0.0)
            bcum = _mm_exact_lhs(incl.astype(BF16), gk)
            tot = jnp.sum(gk, axis=0, keepdims=True)
            e_inv = jnp.exp(-bcum)
            dec = jnp.exp(tot)
            qe = (q * jnp.exp(bcum)).astype(BF16)
            ke = k * e_inv
            kd = (ke * dec).astype(BF16)
            ke = ke.astype(BF16)
            for p in range(GLA_PAIRS):
                sl = slice(p * LANES, (p + 1) * LANES)
                ve = v[:, (2 * p) * GLA_DV:(2 * p + 1) * GLA_DV]
                vo = v[:, (2 * p + 1) * GLA_DV:(2 * p + 2) * GLA_DV]
                units.append(dict(d=d, p=p, sl=sl, incl2=incl2, dec=dec[:, sl], qe=qe[:, sl], ke=ke[:, sl],
                                  kd=kd[:, sl], v2=jnp.concatenate([ve, vo], axis=1),
                                  vbd=jnp.concatenate([jnp.concatenate([ve, zero_v], axis=1),
                                                       jnp.concatenate([zero_v, vo], axis=1)], axis=0)))
        for u in units:
            a = lax.dot_general(u["qe"], _pair_block_diag(u["ke"], even), _NT, preferred_element_type=F32)
            u["a"] = jnp.where(u["incl2"], a, 0.0).astype(BF16)
        for u in units:
            u["st"] = st_ref[u["d"], :, u["sl"]]
            u["o"] = dot(u["a"], u["vbd"]) + lax.dot_general(u["qe"], u["st"].astype(BF16), _NT,
                                                              preferred_element_type=F32)
        for u in units:
            dst = lax.dot_general(u["v2"], u["kd"], _TN, preferred_element_type=F32)
            st_ref[u["d"], :, u["sl"]] = u["st"] * u["dec"] + jnp.where(diag_blocks, dst, 0.0)
        for d, o_ref in enumerate((of_ref, ob_ref)):
            o_ref[0] = jnp.concatenate([u["o"] for u in units if u["d"] == d], axis=-1)


def _gla_branch(zgla, gup_pad, gb):
    bsz, lp, zc = zgla.shape
    nchunks = lp // CHUNK
    back = lambda n: _chunk_of(1, n, nchunks)
    full = lambda a: pl.BlockSpec(a.shape, lambda b, n: (0,) * a.ndim)
    out_f = pl.BlockSpec((1, CHUNK, BRANCH_W), lambda b, n: (b, n, 0))
    out_b = pl.BlockSpec((1, CHUNK, BRANCH_W), lambda b, n: (b, back(n), 0))
    sd = jax.ShapeDtypeStruct((bsz, lp, BRANCH_W), F32)
    return pl.pallas_call(
        functools.partial(_gla_kernel, nchunks=nchunks),
        grid=(bsz, nchunks),
        in_specs=[
            pl.BlockSpec((1, CHUNK, zc), lambda b, n: (b, n, 0)),
            pl.BlockSpec((1, CHUNK, zc), lambda b, n: (b, back(n), 0)),
            full(gup_pad), full(gb),
        ],
        out_specs=[out_f, out_b],
        out_shape=[sd, sd],
        scratch_shapes=[pltpu.VMEM((2, 2 * GLA_DV, GLA_KW), F32)],
        compiler_params=_cparams(("parallel", "arbitrary")),
        name="gla_scan",
    )(zgla, zgla, gup_pad, gb)


HALO = 8


def _rw_prep_kernel(*refs, mix):
    if mix:
        (z_ref, zp_ref, zn_ref, vf_ref, conv_ref, w0_ref, wup_ref, a0_ref, aup_ref, kk_ref, ka_ref,
         rk_ref, ones_ref, v0_ref, vdn_ref, vup_ref,
         r_o, k_o, v_o, kkn_o, lw_o, al_o, bonus_o, sc_ref) = refs
    else:
        (z_ref, zp_ref, zn_ref, conv_ref, w0_ref, wup_ref, a0_ref, aup_ref, kk_ref, ka_ref,
         rk_ref, ones_ref,
         r_o, k_o, v_o, kkn_o, lw_o, al_o, bonus_o, sc_ref) = refs
    i = pl.program_id(1)
    last = pl.num_programs(1) - 1
    nsh = 3 * BRANCH_W
    sc_ref[HALO:HALO + ROW_TILE, 0:nsh] = z_ref[0, :, 0:nsh]
    sc_ref[HALO:HALO + ROW_TILE, nsh:] = z_ref[0, :, nsh + BRANCH_W:]
    sc_ref[0:HALO, 0:nsh] = zp_ref[0, :, 0:nsh]
    sc_ref[0:HALO, nsh:] = zp_ref[0, :, nsh + BRANCH_W:]
    nxt_keep = jnp.where(i == last, 0.0, 1.0)
    sc_ref[HALO + ROW_TILE:, 0:nsh] = zn_ref[0, :, 0:nsh] * nxt_keep
    sc_ref[HALO + ROW_TILE:, nsh:] = zn_ref[0, :, nsh + BRANCH_W:] * nxt_keep
    zm = sc_ref[HALO - 1:HALO - 1 + ROW_TILE, :]
    z0 = sc_ref[HALO:HALO + ROW_TILE, :]
    zp = sc_ref[HALO + 1:HALO + 1 + ROW_TILE, :]
    cw = conv_ref[...]
    zs = zm * cw[0:1] + z0 * cw[1:2] + zp * cw[2:3]
    r = zs[:, 0:BRANCH_W]
    k = zs[:, BRANCH_W:2 * BRANCH_W]
    v = zs[:, 2 * BRANCH_W:nsh]
    wd = zs[:, nsh:nsh + 2 * RW_W_LORA]
    ad = zs[:, nsh + 2 * RW_W_LORA:]
    rowid = lax.broadcasted_iota(jnp.int32, (ROW_TILE, 1), 0)
    live = jnp.logical_or(i > 0, rowid >= META_ROW0)
    twd = jnp.tanh(wd)
    als = []
    for d in range(2):
        wl = _mm3(twd, wup_ref[d]) + w0_ref[d]
        lw = -jnp.exp(-_softplus(-wl) - 0.5)
        lw_o[d, 0] = jnp.where(live, lw, 0.0)
        al = _sigmoid(_mm3(ad, aup_ref[d]) + a0_ref[d])
        al_o[d, 0] = al
        als.append(al)
    kkr = k * kk_ref[...]
    nrm2 = _mm_exact_rhs(kkr * kkr, ones_ref[...])
    kkn = kkr / jnp.maximum(jnp.sqrt(nrm2), 1e-12)
    ka = ka_ref[...]
    kmod0 = k * (1.0 + (als[0] - 1.0) * ka)
    kmod1 = k * (1.0 + (als[1] - 1.0) * ka)
    if mix:
        gate = _sigmoid(v0_ref[...] + _mm3(_mm3(v, vdn_ref[...]), vup_ref[...]))
        v = v + (vf_ref[0] - v) * gate
    kb = 0.5 * (kmod0 + kmod1)
    bsum = _mm_exact_rhs(r * kb * rk_ref[...], ones_ref[...])
    bonus_o[0] = bsum * v
    r_o[0] = r
    k_o[0] = jnp.where(live, k, 0.0)
    v_o[0] = jnp.where(live, v, 0.0)
    kkn_o[0] = jnp.where(live, kkn, 0.0)


def _rw_prep(zrw, vfirst, conv_w, w0, wup_pad, a0, aup_pad, k_k, k_a, r_k, ones_bd, vmix):
    bsz, lp, zc = zrw.shape
    nb = lp // ROW_TILE
    nhalo = lp // HALO
    per_tile = ROW_TILE // HALO
    mix = vmix is not None
    tile = pl.BlockSpec((1, ROW_TILE, BRANCH_W), lambda b, i: (b, i, 0))
    dtile = pl.BlockSpec((2, 1, ROW_TILE, BRANCH_W), lambda b, i: (0, b, i, 0))
    full = lambda a: pl.BlockSpec(a.shape, lambda b, i: (0,) * a.ndim)
    in_specs = [
        pl.BlockSpec((1, ROW_TILE, zc), lambda b, i: (b, i, 0)),
        pl.BlockSpec((1, HALO, zc), lambda b, i: (b, jnp.maximum(i * per_tile - 1, 0), 0)),
        pl.BlockSpec((1, HALO, zc), lambda b, i: (b, jnp.minimum((i + 1) * per_tile, nhalo - 1), 0)),
    ]
    args = [zrw, zrw, zrw]
    if mix:
        in_specs.append(tile)
        args.append(vfirst)
    params = [conv_w, w0, wup_pad, a0, aup_pad, k_k, k_a, r_k, ones_bd]
    if mix:
        params += list(vmix)
    in_specs += [full(p) for p in params]
    args += params
    sd = jax.ShapeDtypeStruct((bsz, lp, BRANCH_W), F32)
    sd2 = jax.ShapeDtypeStruct((2, bsz, lp, BRANCH_W), F32)
    return pl.pallas_call(
        functools.partial(_rw_prep_kernel, mix=mix),
        grid=(bsz, nb),
        in_specs=in_specs,
        out_specs=[tile, tile, tile, tile, dtile, dtile, tile],
        out_shape=[sd, sd, sd, sd, sd2, sd2, sd],
        scratch_shapes=[pltpu.VMEM((ROW_TILE + 2 * HALO, 3 * BRANCH_W + RW_LORA_COLS), F32)],
        compiler_params=_cparams(("parallel", "parallel")),
        name="rw_prep",
    )(*args)


RW_PAIRS = RW_HEADS // 2


def _pair_block_diag(x, even):
    z = jnp.zeros_like(x)
    return jnp.concatenate([jnp.where(even, x, z), jnp.where(even, z, x)], axis=0)


def _rw_scan_kernel(rf_ref, kf_ref, vf_ref, kkf_ref, lwf_ref, alf_ref,
                    rb_ref, kb_ref, vb_ref, kkb_ref, lwb_ref, alb_ref, ka_ref,
                    yf_ref, yb_ref, ht_ref, *, nchunks):
    n = pl.program_id(0)
    bsz = yf_ref.shape[0]

    @pl.when(n == 0)
    def _init():
        ht_ref[...] = jnp.zeros_like(ht_ref)

    @pl.when(n < META_CHUNK)
    def _padding():
        yf_ref[...] = jnp.zeros_like(yf_ref)
        yb_ref[...] = jnp.zeros_like(yb_ref)

    @pl.when(n >= META_CHUNK)
    def _chunk():
        lane = lax.broadcasted_iota(jnp.int32, (1, LANES), 1)
        even = lane < RW_N
        row = lax.broadcasted_iota(jnp.int32, (2 * RW_N, LANES), 0)
        col = lax.broadcasted_iota(jnp.int32, (2 * RW_N, LANES), 1)
        diag_blocks = (row < RW_N) == (col < RW_N)
        t_row = lax.broadcasted_iota(jnp.int32, (CHUNK, LANES), 0)
        t_col = lax.broadcasted_iota(jnp.int32, (CHUNK, LANES), 1) % CHUNK
        eye2 = (t_row == t_col).astype(F32)
        ka = ka_ref[...]
        bd = functools.partial(_pair_block_diag, even=even)

        units = []
        for d, refs in enumerate(((rf_ref, kf_ref, vf_ref, kkf_ref, lwf_ref, alf_ref),
                                  (rb_ref, kb_ref, vb_ref, kkb_ref, lwb_ref, alb_ref))):
            r_ref, k_ref, v_ref, kk_ref, lw_ref, al_ref = refs
            rev = jnp.logical_and(d == 1, n > META_CHUNK)
            incl_m, _ = _order_masks(rev)
            ahead2 = (t_col - t_row) * jnp.where(rev, -1, 1)
            for b in range(bsz):
                lw = lw_ref[0, b]
                al = al_ref[0, b]
                kk = kk_ref[b]
                incl = _mm_exact_lhs(incl_m.astype(BF16), lw)
                tot = jnp.sum(lw, axis=0, keepdims=True)
                e_inv = jnp.exp(-incl)
                ptot = jnp.exp(tot)
                kmod = k_ref[b] * (1.0 + (al - 1.0) * ka)
                bt = (kk * al) * e_inv
                kt = kmod * e_inv
                common = dict(
                    d=d, b=b, incl2=ahead2 <= 0, strict2=ahead2 < 0, ptot=ptot,
                    at=(-kk * jnp.exp(incl - lw)).astype(BF16),
                    rt=(r_ref[b] * jnp.exp(incl)).astype(BF16),
                    bt=bt.astype(BF16), kt=kt.astype(BF16),
                    bp=(bt * ptot).astype(BF16), kp=(kt * ptot).astype(BF16),
                    v=v_ref[b].astype(BF16))
                for p in range(RW_PAIRS):
                    units.append(dict(common, p=p, sl=slice(p * LANES, (p + 1) * LANES)))

        dot = functools.partial(jnp.dot, preferred_element_type=F32)
        for u in units:
            sl = u["sl"]
            lhs = jnp.concatenate([u["at"][:, sl], u["rt"][:, sl]], axis=0)
            rhs = jnp.concatenate([bd(u["bt"][:, sl]), bd(u["kt"][:, sl])], axis=0)
            g = lax.dot_general(lhs, rhs, _NT, preferred_element_type=F32)
            u["n"] = jnp.where(u["strict2"], g[:CHUNK, :LANES], 0.0)
            a_ak = jnp.where(u["strict2"], g[:CHUNK, LANES:], 0.0)
            u["a_r"] = jnp.where(jnp.concatenate([u["incl2"], u["incl2"]], axis=1), g[CHUNK:], 0.0).astype(BF16)
            u["a_ak"] = a_ak.astype(BF16)
        for u in units:
            u["av"] = dot(u["a_ak"], bd(u["v"][:, u["sl"]]))
        steps = int(math.log2(CHUNK))
        for u in units:
            u["t"] = eye2 + u["n"]
        for s in range(steps):
            for u in units:
                nb = u["n"].astype(BF16)
                if s == 0:
                    u["n"] = dot(nb, bd(nb))
                elif s < steps - 1:
                    zz = dot(nb, jnp.concatenate([bd(u["t"].astype(BF16)), bd(nb)], axis=1))
                    u["t"] = u["t"] + zz[:, :LANES]
                    u["n"] = zz[:, LANES:]
                else:
                    u["t"] = u["t"] + dot(nb, bd(u["t"].astype(BF16)))
        for u in units:
            rhs = jnp.concatenate([bd(u["at"][:, u["sl"]]), bd(u["av"].astype(BF16))], axis=1)
            u["wx"] = dot(u["t"].astype(BF16), rhs)
        for u in units:
            sl = u["sl"]
            u["h"] = ht_ref[u["b"], u["d"], :, sl]
            lhs = jnp.concatenate([u["wx"][:, :LANES].astype(BF16), u["rt"][:, sl]], axis=0)
            ru = lax.dot_general(lhs, u["h"].astype(BF16), _NT, preferred_element_type=F32)
            u["u"] = (ru[:CHUNK] + u["wx"][:, LANES:]).astype(BF16)
            u["rh"] = ru[CHUNK:]
        for u in units:
            sl = u["sl"]
            rhs = jnp.concatenate([bd(u["u"]), bd(u["v"][:, sl])], axis=0)
            u["y"] = u["rh"] + dot(u["a_r"], rhs)
            dh = lax.dot_general(jnp.concatenate([u["u"], u["v"][:, sl]], axis=0),
                                 jnp.concatenate([u["bp"][:, sl], u["kp"][:, sl]], axis=0),
                                 _TN, preferred_element_type=F32)
            ht_ref[u["b"], u["d"], :, sl] = u["h"] * u["ptot"][:, sl] + jnp.where(diag_blocks, dh, 0.0)
        for d, y_ref in enumerate((yf_ref, yb_ref)):
            for b in range(bsz):
                y_ref[b] = jnp.concatenate([u["y"] for u in units if u["d"] == d and u["b"] == b], axis=-1)


def _rw_scan(r, k, v, kk, lw, al, k_a):
    bsz, lp, _ = r.shape
    nchunks = lp // CHUNK
    back = lambda n: _chunk_of(1, n, nchunks)
    tok_f = pl.BlockSpec((bsz, CHUNK, BRANCH_W), lambda n: (0, n, 0))
    tok_b = pl.BlockSpec((bsz, CHUNK, BRANCH_W), lambda n: (0, back(n), 0))
    dir_f = pl.BlockSpec((1, bsz, CHUNK, BRANCH_W), lambda n: (0, 0, n, 0))
    dir_b = pl.BlockSpec((1, bsz, CHUNK, BRANCH_W), lambda n: (1, 0, back(n), 0))
    sd = jax.ShapeDtypeStruct((bsz, lp, BRANCH_W), F32)
    yf, yb = pl.pallas_call(
        functools.partial(_rw_scan_kernel, nchunks=nchunks),
        grid=(nchunks,),
        in_specs=[tok_f, tok_f, tok_f, tok_f, dir_f, dir_f,
                  tok_b, tok_b, tok_b, tok_b, dir_b, dir_b,
                  pl.BlockSpec((1, BRANCH_W), lambda n: (0, 0))],
        out_specs=[tok_f, tok_b],
        out_shape=[sd, sd],
        scratch_shapes=[pltpu.VMEM((bsz, 2, 2 * RW_N, BRANCH_W), F32)],
        compiler_params=_cparams(("arbitrary",)),
        name="rw_scan",
    )(r, k, v, kk, lw, al, r, k, v, kk, lw, al, k_a)
    return yf, yb


def _out_kernel(h_ref, ona_ref, ogf_ref, ogb_ref, gg_ref, yf_ref, yb_ref, bonus_ref, gate_ref, wout_ref, gng_ref,
                avg_ref, lng_ref, lnb_ref, fin_ref, o_ref, *, final, tile_off):
    i = pl.program_id(1) + tile_off
    og = ogf_ref[0] + ogb_ref[0]
    parts = []
    for h in range(GLA_HEADS):
        oh = og[:, h * GLA_DV:(h + 1) * GLA_DV]
        ms = jnp.mean(oh * oh, axis=-1, keepdims=True)
        parts.append(oh * lax.rsqrt(ms + GLA_NORM_EPS) * gng_ref[...])
    o_gla = jnp.concatenate(parts, axis=-1) * _silu(gg_ref[0])
    y = yf_ref[0] + yb_ref[0]
    mu = _mm_exact_rhs(y, avg_ref[...])
    yc = y - mu
    var = _mm_exact_rhs(yc * yc, avg_ref[...])
    yn = yc * lax.rsqrt(var + RW_GN_EPS) * lng_ref[...] + lnb_ref[...]
    o_rw = (yn + bonus_ref[0]) * _silu(gate_ref[0])
    cat = jnp.concatenate([ona_ref[0].astype(BF16), o_gla.astype(BF16), o_rw.astype(BF16)], axis=-1)
    hn = h_ref[0] + jnp.dot(cat, wout_ref[...], preferred_element_type=F32)
    if final:
        ms = jnp.mean(hn * hn, axis=-1, keepdims=True)
        o_ref[0] = hn * lax.rsqrt(ms + NORM_EPS) * fin_ref[...]
    else:
        rowid = lax.broadcasted_iota(jnp.int32, (ROW_TILE, 1), 0)
        live = jnp.logical_or(i > 0, rowid >= META_ROW0)
        o_ref[0] = jnp.where(live, hn, 0.0)


def _out_proj(hp, ona, ogla, zgla, yrw, bonus, zrw, wout, gla_norm_g, avg_bd, ln_g, ln_b, fin_g, final):
    bsz, lp, _ = hp.shape
    nb = lp // ROW_TILE
    off = 1 if final else 0
    nt = nb - off
    tile = lambda w, cb: pl.BlockSpec((1, ROW_TILE, w), lambda b, i: (b, i + off, cb))
    full = lambda a: pl.BlockSpec(a.shape, lambda b, i: (0,) * a.ndim)
    gla_gate_blk = (2 * GLA_KW + BRANCH_W) // BRANCH_W
    rw_gate_blk = 3
    out_rows = lp - off * ROW_TILE
    return pl.pallas_call(
        functools.partial(_out_kernel, final=final, tile_off=off),
        grid=(bsz, nt),
        in_specs=[
            tile(D_MODEL, 0), tile(BRANCH_W, 0), tile(BRANCH_W, 0), tile(BRANCH_W, 0),
            tile(BRANCH_W, gla_gate_blk),
            tile(BRANCH_W, 0), tile(BRANCH_W, 0), tile(BRANCH_W, 0), tile(BRANCH_W, rw_gate_blk),
            full(wout), full(gla_norm_g), full(avg_bd), full(ln_g), full(ln_b), full(fin_g),
        ],
        out_specs=pl.BlockSpec((1, ROW_TILE, D_MODEL), lambda b, i: (b, i, 0)),
        out_shape=jax.ShapeDtypeStruct((bsz, out_rows, D_MODEL), F32),
        compiler_params=_cparams(("parallel", "parallel")),
        name="out_proj",
    )(hp, ona, ogla[0], ogla[1], zgla, yrw[0], yrw[1], bonus, zrw, wout, gla_norm_g, avg_bd, ln_g, ln_b, fin_g)


def _block_diag(n, blk, val):
    idx = np.arange(n) // blk
    return jnp.asarray((idx[:, None] == idx[None, :]).astype(np.float32) * val)


def _pad_rows_per_dir(w, rows_total):
    _, r, c = w.shape
    out = jnp.zeros((2, rows_total, c), w.dtype)
    out = out.at[0, 0:r].set(w[0])
    out = out.at[1, r:2 * r].set(w[1])
    return out


def kernel(x, meta, norm_g, w_in, w_out, na_rpb, gla_g_up, gla_g_b, gla_norm_g, rw_conv, rw_w0, rw_w_up,
           rw_a0, rw_a_up, rw_k_k, rw_k_a, rw_r_k, rw_ln_g, rw_ln_b, rw_v0, rw_v_down, rw_v_up, final_norm_g):
    bsz, t, dm = x.shape
    assert dm == D_MODEL and t % ROW_TILE == 0 and t // GRID_W >= WIN_R
    depth = w_in.shape[0]
    lp = t + PAD
    rows = t // GRID_W
    hp = jnp.concatenate([
        jnp.zeros((bsz, META_ROW0, dm), x.dtype),
        jnp.broadcast_to(meta[None].astype(x.dtype), (bsz, N_META, dm)),
        x], axis=1)
    ones_bd = _block_diag(BRANCH_W, RW_N, 1.0).astype(BF16)
    avg_bd = _block_diag(BRANCH_W, RW_N, 1.0 / RW_N).astype(BF16)
    nsh = 3 * BRANCH_W
    vfirst = None
    out = None
    for l in range(depth):
        w = w_in[l]
        wna = w[:, :NA_COLS].astype(BF16)
        wg = w[:, NA_COLS:NA_COLS + GLA_COLS]
        wgla = jnp.pad(wg, ((0, 0), (0, GLA_ZCOLS - GLA_COLS))).astype(BF16)
        wr = w[:, NA_COLS + GLA_COLS:]
        wrw = jnp.concatenate([wr[:, :nsh], wr[:, RW_SHIFT_COLS:], wr[:, nsh:RW_SHIFT_COLS]], axis=1).astype(BF16)
        conv_w = rw_conv[l]
        zna, zgla, zrw = _in_proj(hp.reshape(bsz * lp, dm), norm_g[l][None], wna, wgla, wrw)
        zna = zna.reshape(bsz, lp, -1)
        zgla = zgla.reshape(bsz, lp, -1)
        zrw = zrw.reshape(bsz, lp, -1)

        ona = _na_branch(zna, _na_bias_table(na_rpb[l]), rows)

        gup_pad = _pad_rows_per_dir(gla_g_up[l], LANES)
        ogla = _gla_branch(zgla, gup_pad, gla_g_b[l][:, None, :])

        wup_pad = _pad_rows_per_dir(rw_w_up[l], 2 * RW_W_LORA)
        aup_pad = _pad_rows_per_dir(rw_a_up[l], 2 * RW_A_LORA)
        if l == 0:
            vmix = None
        else:
            vdn = jnp.pad(rw_v_down[l - 1], ((0, 0), (0, LANES - RW_V_LORA)))
            vup = jnp.pad(rw_v_up[l - 1], ((0, LANES - RW_V_LORA), (0, 0)))
            vmix = (rw_v0[l - 1][None], vdn, vup)
        r_, k_, v_, kk_, lw_, al_, bonus = _rw_prep(
            zrw, vfirst, conv_w, rw_w0[l][:, None, :], wup_pad, rw_a0[l][:, None, :], aup_pad,
            rw_k_k[l][None], rw_k_a[l][None], rw_r_k[l][None], ones_bd, vmix)
        if l == 0:
            vfirst = v_
        yrw = _rw_scan(r_, k_, v_, kk_, lw_, al_, rw_k_a[l][None])

        final = l == depth - 1
        res = _out_proj(hp, ona, ogla, zgla, yrw, bonus, zrw, w_out[l].astype(BF16), gla_norm_g[l][None],
                        avg_bd, rw_ln_g[l][None], rw_ln_b[l][None], final_norm_g[None], final)
        if final:
            out = res
        else:
            hp = res
    return out
```
